```python
import math
import jax, jax.numpy as jnp
from jax import lax
import numpy as np

D_MODEL = 2048
BATCH = 4
SEQ = 8192
DEPTH = 1

ATTN_WIDTH = D_MODEL // 2
HEAD_DIM = 64
N_DIFF_HEADS = ATTN_WIDTH // (2 * HEAD_DIM)
SSM_WIDTH = D_MODEL // 2
SSM_GROUP = 16
N_SSM_GROUPS = SSM_WIDTH // SSM_GROUP
SSM_STATE = 64
D_FF = -(-8 * D_MODEL // (3 * 256)) * 256
IN_COLS = 3 * ATTN_WIDTH + SSM_WIDTH + 2 * D_MODEL
Q_BLOCK = 128
EPS = 1e-6
DT_MIN = 1e-3
DT_MAX = 1e-1
NEG_INF = -1e30

kernel_name = "hybrid_diffattn_s5_gated_block"


def _rms_norm(x, w):
    xf = x.astype(jnp.float32)
    y = xf * lax.rsqrt(jnp.mean(xf * xf, axis=-1, keepdims=True) + EPS)
    return (y * w.astype(jnp.float32)).astype(x.dtype)


def _lambda_init(layer_idx):
    return 0.8 - 0.6 * math.exp(-0.3 * layer_idx)


def _diff_attention(q, k, v, lam, lam_init, subln_w):
    bsz, seq, n_heads, _, dh = q.shape
    e = v.shape[-1]
    scale = dh ** -0.5
    n_blocks = seq // Q_BLOCK
    kpos = jnp.arange(seq)

    def one_block(i):
        start = i * Q_BLOCK
        qb = lax.dynamic_slice_in_dim(q, start, Q_BLOCK, axis=1)
        s = jnp.einsum('bqhcd,bkhcd->bhcqk', qb, k).astype(jnp.float32) * scale
        qpos = start + jnp.arange(Q_BLOCK)
        causal = kpos[None, :] <= qpos[:, None]
        s = jnp.where(causal, s, NEG_INF)
        p = jax.nn.softmax(s, axis=-1)
        a = p[:, :, 0] - lam * p[:, :, 1]
        return jnp.einsum('bhqk,bkhe->bqhe', a.astype(v.dtype), v)

    o = lax.map(one_block, jnp.arange(n_blocks))
    o = jnp.moveaxis(o, 0, 1).reshape(bsz, seq, n_heads, e)
    o = _rms_norm(o, subln_w) * (1.0 - lam_init)
    return o.reshape(bsz, seq, n_heads * e)


def _ssm_binop(left, right):
    a1, b1 = left
    a2, b2 = right
    return a1 * a2, a2 * b1 + b2


def _s5_groups(u, a_re, a_im, log_dt, b_re, b_im, c_re, c_im, d_skip):
    f32 = jnp.float32
    lam = lax.complex(a_re.astype(f32), a_im.astype(f32))
    dt = jnp.exp(log_dt.astype(f32))[:, None]
    a_bar = jnp.exp(lam * dt)
    b = lax.complex(b_re.astype(f32), b_im.astype(f32))
    b_bar = ((a_bar - 1.0) / lam)[..., None] * b
    c = lax.complex(c_re.astype(f32), c_im.astype(f32))
    d = d_skip.astype(f32)

    def one_sequence(u_seq):
        uf = u_seq.astype(f32)
        bu = jnp.einsum('gph,lgh->lgp', b_bar, uf.astype(jnp.complex64))
        a_seq = jnp.broadcast_to(a_bar, bu.shape)
        _, states = lax.associative_scan(_ssm_binop, (a_seq, bu), axis=0)
        y = jnp.einsum('ghp,lgp->lgh', c, states).real + d * uf
        return y.astype(u_seq.dtype)

    return lax.map(one_sequence, u)


def setup_inputs(seed: int = 0) -> dict:
    key = jax.random.key(seed)
    ks = jax.random.split(key, 32)
    f32 = jnp.float32

    def nrm(k, shape, scale):
        return jax.random.normal(k, shape, f32) * scale

    def gain(k, shape):
        return 1.0 + 0.01 * jax.random.normal(k, shape, f32)

    L_, G, P, H = DEPTH, N_SSM_GROUPS, SSM_STATE, SSM_GROUP
    x = jax.random.normal(ks[0], (BATCH, SEQ, D_MODEL), f32)
    a_re = -0.5 + 0.01 * jax.random.normal(ks[1], (L_, G, P), f32)
    a_im = math.pi * jnp.arange(P, dtype=f32)[None, None, :] + 0.01 * jax.random.normal(ks[2], (L_, G, P), f32)
    log_dt = jax.random.uniform(ks[3], (L_, G), f32, math.log(DT_MIN), math.log(DT_MAX))
    return {
        "x": x,
        "w_in": nrm(ks[4], (L_, D_MODEL, IN_COLS), D_MODEL ** -0.5),
        "lambda_q1": nrm(ks[5], (L_, HEAD_DIM), 0.1),
        "lambda_k1": nrm(ks[6], (L_, HEAD_DIM), 0.1),
        "lambda_q2": nrm(ks[7], (L_, HEAD_DIM), 0.1),
        "lambda_k2": nrm(ks[8], (L_, HEAD_DIM), 0.1),
        "subln_w": gain(ks[9], (L_, 2 * HEAD_DIM)),
        "ssm_a_re": a_re,
        "ssm_a_im": a_im,
        "ssm_log_dt": log_dt,
        "ssm_b_re": nrm(ks[10], (L_, G, P, H), (2 * H) ** -0.5),
        "ssm_b_im": nrm(ks[11], (L_, G, P, H), (2 * H) ** -0.5),
        "ssm_c_re": nrm(ks[12], (L_, G, H, P), P ** -0.5),
        "ssm_c_im": nrm(ks[13], (L_, G, H, P), P ** -0.5),
        "ssm_d": nrm(ks[14], (L_, G, H), 1.0),
        "w_glu": nrm(ks[15], (L_, SSM_WIDTH, SSM_WIDTH), SSM_WIDTH ** -0.5),
        "b_glu": nrm(ks[16], (L_, SSM_WIDTH), 0.01),
        "w_attn_branch": nrm(ks[17], (L_, ATTN_WIDTH, D_MODEL), ATTN_WIDTH ** -0.5),
        "w_ssm_branch": nrm(ks[18], (L_, SSM_WIDTH, D_MODEL), SSM_WIDTH ** -0.5),
        "w_out": nrm(ks[19], (L_, D_MODEL, D_MODEL), D_MODEL ** -0.5),
        "norm_mix_pre": gain(ks[20], (L_, D_MODEL)),
        "norm_mix_post": gain(ks[21], (L_, D_MODEL)),
        "w_ffn_gate": nrm(ks[22], (L_, D_MODEL, D_FF), D_MODEL ** -0.5),
        "w_ffn_up": nrm(ks[23], (L_, D_MODEL, D_FF), D_MODEL ** -0.5),
        "w_ffn_down": nrm(ks[24], (L_, D_FF, D_MODEL), D_FF ** -0.5),
        "norm_ffn_pre": gain(ks[25], (L_, D_MODEL)),
        "norm_ffn_post": gain(ks[26], (L_, D_MODEL)),
    }


def reference(x, w_in, lambda_q1, lambda_k1, lambda_q2, lambda_k2, subln_w,
              ssm_a_re, ssm_a_im, ssm_log_dt, ssm_b_re, ssm_b_im, ssm_c_re, ssm_c_im,
              ssm_d, w_glu, b_glu, w_attn_branch, w_ssm_branch, w_out,
              norm_mix_pre, norm_mix_post, w_ffn_gate, w_ffn_up, w_ffn_down,
              norm_ffn_pre, norm_ffn_post):
    bsz, seq, _ = x.shape
    splits = np.cumsum([ATTN_WIDTH, ATTN_WIDTH, ATTN_WIDTH, SSM_WIDTH, D_MODEL]).tolist()
    h = x
    for l in range(DEPTH):
        lam_init = _lambda_init(l)
        u = _rms_norm(h, norm_mix_pre[l])
        proj = u @ w_in[l]
        q, k, v, s_in, g_a, g_s = jnp.split(proj, splits, axis=-1)
        q = q.reshape(bsz, seq, N_DIFF_HEADS, 2, HEAD_DIM)
        k = k.reshape(bsz, seq, N_DIFF_HEADS, 2, HEAD_DIM)
        v = v.reshape(bsz, seq, N_DIFF_HEADS, 2 * HEAD_DIM)
        f32 = jnp.float32
        lam = (jnp.exp(jnp.sum(lambda_q1[l].astype(f32) * lambda_k1[l].astype(f32)))
               - jnp.exp(jnp.sum(lambda_q2[l].astype(f32) * lambda_k2[l].astype(f32)))
               + lam_init)
        y_a = _diff_attention(q, k, v, lam, lam_init, subln_w[l])

        s_u = s_in.reshape(bsz, seq, N_SSM_GROUPS, SSM_GROUP)
        y_s = _s5_groups(s_u, ssm_a_re[l], ssm_a_im[l], ssm_log_dt[l], ssm_b_re[l],
                         ssm_b_im[l], ssm_c_re[l], ssm_c_im[l], ssm_d[l])
        y_s = jax.nn.gelu(y_s.reshape(bsz, seq, SSM_WIDTH))
        y_s = y_s * jax.nn.sigmoid(y_s @ w_glu[l] + b_glu[l])

        merged = (jax.nn.sigmoid(g_a) * (y_a @ w_attn_branch[l])
                  + jax.nn.sigmoid(g_s) * (y_s @ w_ssm_branch[l]))
        h = h + _rms_norm(merged @ w_out[l], norm_mix_post[l])
        z = _rms_norm(h, norm_ffn_pre[l])
        f = (jax.nn.silu(z @ w_ffn_gate[l]) * (z @ w_ffn_up[l])) @ w_ffn_down[l]
        h = h + _rms_norm(f, norm_ffn_post[l])
    return h
```

```python
import functools
import math

import jax
import jax.numpy as jnp
from jax import lax
from jax.experimental import pallas as pl
from jax.experimental.pallas import tpu as pltpu

EPS = 1e-6
NEG_INF = -1e30
LANES = 128
SSM_CHUNK = 8
SCAN_ROWS = 8
VMEM_LIMIT_BYTES = 56 * 1024 * 1024
F32 = jnp.float32
BF16 = jnp.bfloat16


def _params(*sem):
    return pltpu.CompilerParams(dimension_semantics=sem,
                                vmem_limit_bytes=VMEM_LIMIT_BYTES)


def _rms(x, gain):
    ms = jnp.mean(x * x, axis=-1, keepdims=True)
    return x * lax.rsqrt(ms + EPS) * gain


def _tile(n, pref):
    t = min(n, pref)
    assert n % t == 0, (n, t)
    return t


def _in_proj_kernel(x_ref, g_ref, w_ref, o_ref, u_ref):
    @pl.when(pl.program_id(1) == 0)
    def _():
        u_ref[...] = _rms(x_ref[...], g_ref[...]).astype(u_ref.dtype)

    o_ref[...] = jnp.dot(u_ref[...], w_ref[...],
                         preferred_element_type=F32).astype(o_ref.dtype)


def _in_proj(x2, gain, w):
    t, d = x2.shape
    n = w.shape[1]
    tm = _tile(t, 512)
    tn = _tile(n, 2048)
    return pl.pallas_call(
        _in_proj_kernel,
        grid=(t // tm, n // tn),
        in_specs=[pl.BlockSpec((tm, d), lambda i, j: (i, 0)),
                  pl.BlockSpec((1, d), lambda i, j: (0, 0)),
                  pl.BlockSpec((d, tn), lambda i, j: (0, j))],
        out_specs=pl.BlockSpec((tm, tn), lambda i, j: (i, j)),
        out_shape=jax.ShapeDtypeStruct((t, n), BF16),
        scratch_shapes=[pltpu.VMEM((tm, d), BF16)],
        compiler_params=_params("parallel", "arbitrary"),
        name="in_proj",
    )(x2, gain, w)


def _attn_kernel(q_ref, k_ref, v_ref, lq1_ref, lk1_ref, lq2_ref, lk2_ref, sw_ref,
                 o_ref, qs_ref, m_ref, l_ref, acc_ref, *, tq, tk, dh, lam_init):
    qi = pl.program_id(2)
    nsub = tq // tk

    q = q_ref[...] * jnp.asarray(dh ** -0.5, q_ref.dtype)
    lane = lax.broadcasted_iota(jnp.int32, q.shape, 1)
    zero = jnp.zeros_like(q)
    qs_ref[0:tq, :] = jnp.where(lane < dh, q, zero)
    qs_ref[tq:2 * tq, :] = jnp.where(lane >= dh, q, zero)
    m_ref[...] = jnp.full(m_ref.shape, NEG_INF, F32)
    l_ref[...] = jnp.zeros(l_ref.shape, F32)
    acc_ref[...] = jnp.zeros(acc_ref.shape, F32)

    def step(ki, masked):
        start = pl.multiple_of(ki * tk, tk)
        k = k_ref[pl.ds(start, tk), :]
        v = v_ref[pl.ds(start, tk), :]
        s = lax.dot_general(qs_ref[...], k, (((1,), (1,)), ((), ())),
                            preferred_element_type=F32)
        if masked:
            row = lax.broadcasted_iota(jnp.int32, (2 * tq, tk), 0)
            row = jnp.where(row >= tq, row - tq, row)
            col = lax.broadcasted_iota(jnp.int32, (2 * tq, tk), 1)
            s = jnp.where(ki * tk + col <= qi * tq + row, s, NEG_INF)
        m_prev = m_ref[...]
        m_new = jnp.maximum(m_prev, jnp.max(s, axis=-1, keepdims=True))
        alpha = jnp.exp(m_prev - m_new)
        p = jnp.exp(s - m_new)
        l_ref[...] = alpha * l_ref[...] + jnp.sum(p, axis=-1, keepdims=True)
        acc_ref[...] = alpha * acc_ref[...] + jnp.dot(
            p.astype(v.dtype), v, preferred_element_type=F32)
        m_ref[...] = m_new

    def body(ki, carry):
        step(ki, False)
        return carry

    lax.fori_loop(0, qi * nsub, body, 0)
    for d in range(nsub):
        step(qi * nsub + d, True)

    lam = (jnp.exp(jnp.sum(lq1_ref[...] * lk1_ref[...], axis=-1, keepdims=True))
           - jnp.exp(jnp.sum(lq2_ref[...] * lk2_ref[...], axis=-1, keepdims=True))
           + lam_init)
    o_all = acc_ref[...] / l_ref[...]
    o = o_all[0:tq, :] - lam * o_all[tq:2 * tq, :]
    o = _rms(o, sw_ref[...]) * (1.0 - lam_init)
    o_ref[...] = o.astype(o_ref.dtype)


def _attention(proj3, lq1, lk1, lq2, lk2, subln, *, n_heads, dh, lam_init):
    bsz, seq, _ = proj3.shape
    e = 2 * dh
    tq = _tile(seq, 512)
    tk = _tile(tq, 512)
    kern = functools.partial(_attn_kernel, tq=tq, tk=tk, dh=dh, lam_init=lam_init)
    vec = pl.BlockSpec((1, dh), lambda b, h, i: (0, 0))
    return pl.pallas_call(
        kern,
        grid=(bsz, n_heads, seq // tq),
        in_specs=[pl.BlockSpec((None, tq, e), lambda b, h, i: (b, i, h)),
                  pl.BlockSpec((None, seq, e), lambda b, h, i: (b, 0, n_heads + h)),
                  pl.BlockSpec((None, seq, e), lambda b, h, i: (b, 0, 2 * n_heads + h)),
                  vec, vec, vec, vec,
                  pl.BlockSpec((1, e), lambda b, h, i: (0, 0))],
        out_specs=pl.BlockSpec((None, tq, e), lambda b, h, i: (b, i, h)),
        out_shape=jax.ShapeDtypeStruct((bsz, seq, n_heads * e), BF16),
        scratch_shapes=[pltpu.VMEM((2 * tq, e), BF16),
                        pltpu.VMEM((2 * tq, 1), F32),
                        pltpu.VMEM((2 * tq, 1), F32),
                        pltpu.VMEM((2 * tq, e), F32)],
        compiler_params=_params("parallel", "parallel", "arbitrary"),
        name="diff_attention",
    )(proj3, proj3, proj3, lq1, lk1, lq2, lk2, subln)


def _ssm_tables(a_re, a_im, log_dt, b_re, b_im, c_re, c_im, d_skip, *, seg_rows):
    g, p, h = b_re.shape
    tc = SSM_CHUNK
    gpb = LANES // h
    jb = g // gpb
    dt = jnp.exp(log_dt)[:, None]
    er, ei = a_re * dt, a_im * dt

    def powers(n):
        n = jnp.asarray(n, F32).reshape((-1, 1, 1))
        mag = jnp.exp(n * er)
        return mag * jnp.cos(n * ei), mag * jnp.sin(n * ei)

    ar, ai = powers([1.0])
    ar, ai = ar[0], ai[0]
    den = a_re * a_re + a_im * a_im
    fr = ((ar - 1.0) * a_re + ai * a_im) / den
    fi = (ai * a_re - (ar - 1.0) * a_im) / den
    bbr = fr[..., None] * b_re - fi[..., None] * b_im
    bbi = fr[..., None] * b_im + fi[..., None] * b_re

    pr, pi = powers(jnp.arange(tc + 1))
    cpr = c_re[None] * pr[:, :, None, :] - c_im[None] * pi[:, :, None, :]
    cpi = c_re[None] * pi[:, :, None, :] + c_im[None] * pr[:, :, None, :]
    kern = (jnp.einsum('tgop,gpi->tgio', cpr[:tc], bbr)
            - jnp.einsum('tgop,gpi->tgio', cpi[:tc], bbi))
    kidx = jnp.arange(tc)
    tau = kidx[None, :] - kidx[:, None]
    kt = jnp.where((tau >= 0)[:, :, None, None, None], kern[jnp.clip(tau, 0)], 0.0)
    eye = jnp.eye(gpb, dtype=F32)
    kt = kt.reshape(tc, tc, jb, gpb, h, h)
    t_mat = jnp.einsum('kqjgio,gf->jkgiqfo', kt, eye).reshape(jb, tc * LANES, tc * LANES)

    qr, qi_ = pr[tc - 1 - kidx], pi[tc - 1 - kidx]
    sr = qr[..., None] * bbr[None] - qi_[..., None] * bbi[None]
    si = qr[..., None] * bbi[None] + qi_[..., None] * bbr[None]
    st = jnp.stack([sr, si], 0).reshape(2, tc, jb, gpb, p, h)
    w_st = jnp.einsum('ckjgpi,gf->jkgicfp', st, eye).reshape(jb, tc * LANES, 2 * gpb * p)

    ot = jnp.stack([cpr[1:], -cpi[1:]], 0).reshape(2, tc, jb, gpb, h, p)
    w_out = jnp.einsum('cqjgop,gf->jcgpqfo', ot, eye).reshape(jb, 2 * gpb * p, tc * LANES)

    def rows(n):
        xr, xi = powers(n)
        x = jnp.stack([xr, xi], 1).reshape(-1, 2, jb, gpb * p)
        return jnp.transpose(x, (2, 0, 1, 3)).reshape(jb, -1, 2 * gpb * p)

    a_chunk = rows([float(tc)])
    a_seg = rows([float(tc * seg_rows)])
    p_pow = rows(tc * jnp.arange(seg_rows))
    d_row = jnp.tile(d_skip.reshape(jb, 1, LANES), (1, 1, tc))
    return (t_mat.astype(BF16), w_st.astype(BF16), w_out.astype(BF16),
            a_chunk, a_seg, p_pow, d_row)


def _cmul(ar, ai, xr, xi):
    return ar * xr - ai * xi, ar * xi + ai * xr


def _ssm_kernel(u_ref, t_ref, wst_ref, wout_ref, ach_ref, aseg_ref, pp_ref, d_ref,
                y_ref, xst_ref, e_ref, sp_ref, *, seg_rows):
    ncb = xst_ref.shape[0]
    hb = ncb // 2
    blk = lambda c: slice(c * LANES, (c + 1) * LANES)
    u = u_ref[...]
    xst = jnp.dot(u, wst_ref[...], preferred_element_type=F32)
    for c in range(ncb):
        xst_ref[c] = xst[:, blk(c)]

    ach = ach_ref[...]
    a_ch = [jnp.broadcast_to(ach[:, blk(c)], (SCAN_ROWS, LANES)) for c in range(ncb)]

    def cstep(a, e, c):
        return (a[c] * e[c] - a[c + hb] * e[c + hb], a[c] * e[c + hb] + a[c + hb] * e[c])

    def scan_body(i, e):
        row0 = pl.multiple_of(i * SCAN_ROWS, SCAN_ROWS)
        new = [None] * ncb
        for c in range(hb):
            e_ref[c, pl.ds(row0, SCAN_ROWS), :] = e[c]
            e_ref[c + hb, pl.ds(row0, SCAN_ROWS), :] = e[c + hb]
            nr, ni = cstep(a_ch, e, c)
            new[c] = nr + xst_ref.at[c][pl.ds(i, SCAN_ROWS, stride=seg_rows), :]
            new[c + hb] = ni + xst_ref.at[c + hb][pl.ds(i, SCAN_ROWS, stride=seg_rows), :]
        return tuple(new)

    zero = jnp.zeros((SCAN_ROWS, LANES), F32)
    end = lax.fori_loop(0, seg_rows, scan_body, (zero,) * ncb)

    asg = aseg_ref[...]
    a_sg = [asg[:, blk(c)] for c in range(ncb)]
    car = [jnp.zeros((1, LANES), F32)] * ncb
    pp = pp_ref[...]
    p_pw = [pp[:, blk(c)] for c in range(ncb)]
    for s in range(SCAN_ROWS):
        rows = slice(s * seg_rows, (s + 1) * seg_rows)
        nxt = [None] * ncb
        for c in range(hb):
            fr, fi = cstep(p_pw, car, c)
            sp_ref[rows, blk(c)] = (
                e_ref.at[c][pl.ds(s, seg_rows, stride=SCAN_ROWS), :] + fr).astype(sp_ref.dtype)
            sp_ref[rows, blk(c + hb)] = (
                e_ref.at[c + hb][pl.ds(s, seg_rows, stride=SCAN_ROWS), :] + fi).astype(sp_ref.dtype)
            nr, ni = cstep(a_sg, car, c)
            nxt[c] = nr + end[c][s:s + 1, :]
            nxt[c + hb] = ni + end[c + hb][s:s + 1, :]
        car = nxt

    y = jnp.dot(u, t_ref[...], preferred_element_type=F32)
    y = y + jnp.dot(sp_ref[...], wout_ref[...], preferred_element_type=F32)
    y = y + d_ref[...] * u.astype(F32)
    y_ref[...] = y.astype(y_ref.dtype)


def _ssm(u, tables, *, bsz, seg_rows):
    t_mat, w_st, w_out, a_chunk, a_seg, p_pow, d_row = tables
    jb, m, kc = u.shape
    nc = m // bsz
    s2 = w_st.shape[2]
    assert nc == SCAN_ROWS * seg_rows
    kern = functools.partial(_ssm_kernel, seg_rows=seg_rows)
    wspec = lambda r, c: pl.BlockSpec((None, r, c), lambda j, b: (j, 0, 0))
    return pl.pallas_call(
        kern,
        grid=(jb, bsz),
        in_specs=[pl.BlockSpec((None, nc, kc), lambda j, b: (j, b, 0)),
                  wspec(kc, kc), wspec(kc, s2), wspec(s2, kc),
                  wspec(1, s2), wspec(1, s2), wspec(seg_rows, s2), wspec(1, kc)],
        out_specs=pl.BlockSpec((None, nc, kc), lambda j, b: (j, b, 0)),
        out_shape=jax.ShapeDtypeStruct((jb, m, kc), BF16),
        scratch_shapes=[pltpu.VMEM((s2 // LANES, nc, LANES), F32),
                        pltpu.VMEM((s2 // LANES, nc, LANES), F32),
                        pltpu.VMEM((nc, s2), BF16)],
        compiler_params=_params("parallel", "arbitrary"),
        name="s5_scan",
    )(u, t_mat, w_st, w_out, a_chunk, a_seg, p_pow, d_row)


def _merge_kernel(x_ref, ya_ref, ys_ref, ga_ref, gs_ref, wglu_ref, bglu_ref, wa_ref,
                  ws_ref, wo_ref, npost_ref, npre_ref, h_ref, z_ref):
    ys = jax.nn.gelu(ys_ref[...].astype(F32), approximate=True)
    gate = jnp.dot(ys.astype(BF16), wglu_ref[...], preferred_element_type=F32)
    ys = ys * jax.nn.sigmoid(gate + bglu_ref[...])
    ma = jnp.dot(ya_ref[...], wa_ref[...], preferred_element_type=F32)
    ms = jnp.dot(ys.astype(BF16), ws_ref[...], preferred_element_type=F32)
    merged = (jax.nn.sigmoid(ga_ref[...].astype(F32)) * ma
              + jax.nn.sigmoid(gs_ref[...].astype(F32)) * ms)
    mix = jnp.dot(merged.astype(BF16), wo_ref[...], preferred_element_type=F32)
    h = x_ref[...] + _rms(mix, npost_ref[...])
    h_ref[...] = h
    z_ref[...] = _rms(h, npre_ref[...]).astype(z_ref.dtype)


def _merge(x2, y_a, y_s, proj, w_glu, b_glu, w_a, w_s, w_o, n_post, n_pre, *, gate_col):
    t, d = x2.shape
    wa = y_a.shape[1]
    wsm = y_s.shape[1]
    tm = _tile(t, 256)
    row = lambda c: pl.BlockSpec((tm, c), lambda i: (i, 0))
    full = lambda a: pl.BlockSpec(a.shape, lambda i: (0, 0), pipeline_mode=pl.Buffered(1))
    ga_blk = gate_col // d
    return pl.pallas_call(
        _merge_kernel,
        grid=(t // tm,),
        in_specs=[row(d), row(wa), row(wsm),
                  pl.BlockSpec((tm, d), lambda i: (i, ga_blk)),
                  pl.BlockSpec((tm, d), lambda i: (i, ga_blk + 1)),
                  full(w_glu), full(b_glu), full(w_a), full(w_s), full(w_o),
                  full(n_post), full(n_pre)],
        out_specs=[row(d), row(d)],
        out_shape=[jax.ShapeDtypeStruct((t, d), F32),
                   jax.ShapeDtypeStruct((t, d), BF16)],
        compiler_params=_params("parallel"),
        name="merge",
    )(x2, y_a, y_s, proj, proj, w_glu, b_glu, w_a, w_s, w_o, n_post, n_pre)


def _ffn_kernel(z_ref, h_ref, wg_ref, wu_ref, wd_ref, npost_ref, o_ref, acc_ref):
    j = pl.program_id(1)
    z = z_ref[...]
    g = jnp.dot(z, wg_ref[...], preferred_element_type=F32)
    u = jnp.dot(z, wu_ref[...], preferred_element_type=F32)
    a = (jax.nn.silu(g) * u).astype(BF16)
    part = jnp.dot(a, wd_ref[...], preferred_element_type=F32)

    @pl.when(j == 0)
    def _():
        acc_ref[...] = part

    @pl.when(j > 0)
    def _():
        acc_ref[...] += part

    @pl.when(j == pl.num_programs(1) - 1)
    def _():
        o_ref[...] = h_ref[...] + _rms(acc_ref[...], npost_ref[...])


def _ffn(z, h, w_g, w_u, w_d, n_post):
    t, d = z.shape
    f = w_g.shape[1]
    tm = _tile(t, 512)
    tf = _tile(f, 512)
    return pl.pallas_call(
        _ffn_kernel,
        grid=(t // tm, f // tf),
        in_specs=[pl.BlockSpec((tm, d), lambda i, j: (i, 0)),
                  pl.BlockSpec((tm, d), lambda i, j: (i, 0)),
                  pl.BlockSpec((d, tf), lambda i, j: (0, j)),
                  pl.BlockSpec((d, tf), lambda i, j: (0, j)),
                  pl.BlockSpec((tf, d), lambda i, j: (j, 0)),
                  pl.BlockSpec((1, d), lambda i, j: (0, 0))],
        out_specs=pl.BlockSpec((tm, d), lambda i, j: (i, 0)),
        out_shape=jax.ShapeDtypeStruct((t, d), F32),
        scratch_shapes=[pltpu.VMEM((tm, d), F32)],
        compiler_params=_params("parallel", "arbitrary"),
        name="ffn",
    )(z, h, w_g, w_u, w_d, n_post)


def _lambda_init(layer_idx):
    return 0.8 - 0.6 * math.exp(-0.3 * layer_idx)


def kernel(x, w_in, lambda_q1, lambda_k1, lambda_q2, lambda_k2, subln_w, ssm_a_re, ssm_a_im, ssm_log_dt, ssm_b_re, ssm_b_im, ssm_c_re, ssm_c_im, ssm_d, w_glu, b_glu, w_attn_branch, w_ssm_branch, w_out, norm_mix_pre, norm_mix_post, w_ffn_gate, w_ffn_up, w_ffn_down, norm_ffn_pre, norm_ffn_post):
    bsz, seq, d = x.shape
    depth = w_in.shape[0]
    dh = lambda_q1.shape[-1]
    e = 2 * dh
    attn_w = w_attn_branch.shape[1]
    ssm_w = w_ssm_branch.shape[1]
    n_heads = attn_w // e
    jb = ssm_w // LANES
    tc = SSM_CHUNK
    nc = seq // tc
    seg_rows = nc // SCAN_ROWS
    t = bsz * seq
    row = lambda v: v.reshape(1, -1).astype(F32)

    h = x.reshape(t, d)
    for l in range(depth):
        lam_init = _lambda_init(l)
        proj = _in_proj(h, row(norm_mix_pre[l]), w_in[l].astype(BF16))
        y_a = _attention(proj.reshape(bsz, seq, -1), row(lambda_q1[l]), row(lambda_k1[l]),
                         row(lambda_q2[l]), row(lambda_k2[l]), row(subln_w[l]),
                         n_heads=n_heads, dh=dh, lam_init=lam_init)
        y_a = y_a.reshape(t, attn_w)

        tables = _ssm_tables(ssm_a_re[l], ssm_a_im[l], ssm_log_dt[l], ssm_b_re[l],
                             ssm_b_im[l], ssm_c_re[l], ssm_c_im[l], ssm_d[l],
                             seg_rows=seg_rows)
        s_in = proj[:, 3 * attn_w:3 * attn_w + ssm_w]
        u = s_in.reshape(bsz * nc, tc, jb, LANES).transpose(2, 0, 1, 3)
        u = u.reshape(jb, bsz * nc, tc * LANES)
        y_s = _ssm(u, tables, bsz=bsz, seg_rows=seg_rows)
        y_s = y_s.reshape(jb, bsz * nc, tc, LANES).transpose(1, 2, 0, 3).reshape(t, ssm_w)

        h1, z = _merge(h, y_a, y_s, proj, w_glu[l].astype(BF16), row(b_glu[l]),
                       w_attn_branch[l].astype(BF16), w_ssm_branch[l].astype(BF16),
                       w_out[l].astype(BF16), row(norm_mix_post[l]), row(norm_ffn_pre[l]),
                       gate_col=3 * attn_w + ssm_w)
        h = _ffn(z, h1, w_ffn_gate[l].astype(BF16), w_ffn_up[l].astype(BF16),
                 w_ffn_down[l].astype(BF16), row(norm_ffn_post[l]))
    return h.reshape(bsz, seq, d)
```

```python
import functools
import math

import jax
import jax.numpy as jnp
from jax import lax
from jax.experimental import pallas as pl
from jax.experimental.pallas import tpu as pltpu

EPS = 1e-6
NEG_INF = -1e30
LANES = 128
SSM_CHUNK = 8
SCAN_ROWS = 8
VMEM_LIMIT_BYTES = 56 * 1024 * 1024
F32 = jnp.float32
BF16 = jnp.bfloat16
_NT = (((1,), (1,)), ((), ()))


def _params(*sem):
    return pltpu.CompilerParams(dimension_semantics=sem,
                                vmem_limit_bytes=VMEM_LIMIT_BYTES)


def _rms(x, gain):
    ms = jnp.mean(x * x, axis=-1, keepdims=True)
    return x * lax.rsqrt(ms + EPS) * gain


def _tile(n, pref):
    t = min(n, pref)
    assert n % t == 0, (n, t)
    return t


def _in_proj_kernel(x_ref, g_ref, w_ref, wvt_ref, o_ref, vt_ref, u_ref):
    @pl.when(pl.program_id(1) == 0)
    def _():
        u = _rms(x_ref[...], g_ref[...]).astype(u_ref.dtype)
        u_ref[...] = u
        vt_ref[...] = lax.dot_general(wvt_ref[...], u, _NT,
                                      preferred_element_type=F32).astype(vt_ref.dtype)

    o_ref[...] = jnp.dot(u_ref[...], w_ref[...],
                         preferred_element_type=F32).astype(o_ref.dtype)


def _in_proj(x2, gain, w, wvt, *, tm, tn):
    t, d = x2.shape
    n = w.shape[1]
    nv = wvt.shape[0]
    return pl.pallas_call(
        _in_proj_kernel,
        grid=(t // tm, n // tn),
        in_specs=[pl.BlockSpec((tm, d), lambda i, j: (i, 0)),
                  pl.BlockSpec((1, d), lambda i, j: (0, 0)),
                  pl.BlockSpec((d, tn), lambda i, j: (0, j)),
                  pl.BlockSpec((nv, d), lambda i, j: (0, 0), pipeline_mode=pl.Buffered(1))],
        out_specs=[pl.BlockSpec((tm, tn), lambda i, j: (i, j)),
                   pl.BlockSpec((None, nv, tm), lambda i, j: (i, 0, 0))],
        out_shape=[jax.ShapeDtypeStruct((t, n), BF16),
                   jax.ShapeDtypeStruct((t // tm, nv, tm), BF16)],
        scratch_shapes=[pltpu.VMEM((tm, d), BF16)],
        compiler_params=_params("parallel", "arbitrary"),
        name="in_proj",
    )(x2, gain, w, wvt)


def _attn_kernel(q_ref, k_ref, vt_ref, lq1_ref, lk1_ref, lq2_ref, lk2_ref, sw_ref,
                 o_ref, qs_ref, m_ref, l_ref, acc_ref, *, tq, tk, dh, lam_init):
    qi = pl.program_id(2)

    q = q_ref[...]
    lane = lax.broadcasted_iota(jnp.int32, q.shape, 1)
    zero = jnp.zeros_like(q)
    qs_ref[0:tq, :] = jnp.where(lane < dh, q, zero)
    qs_ref[tq:2 * tq, :] = jnp.where(lane >= dh, q, zero)
    m_ref[...] = jnp.full(m_ref.shape, NEG_INF, F32)
    l_ref[...] = jnp.zeros(l_ref.shape, F32)
    acc_ref[...] = jnp.zeros(acc_ref.shape, F32)

    def step(ki, masked):
        k = k_ref[pl.ds(pl.multiple_of(ki * tk, tk), tk), :]
        vt = vt_ref[ki]
        s = lax.dot_general(k, qs_ref[...], _NT, preferred_element_type=F32)
        if masked:
            kpos = ki * tk + lax.broadcasted_iota(jnp.int32, s.shape, 0)
            col = lax.broadcasted_iota(jnp.int32, s.shape, 1)
            qpos = qi * tq + jnp.where(col >= tq, col - tq, col)
            s = jnp.where(kpos <= qpos, s, NEG_INF)
        m_prev = m_ref[...]
        m_new = jnp.maximum(m_prev, jnp.max(s, axis=0, keepdims=True))
        alpha = jnp.exp2(m_prev - m_new)
        p = jnp.exp2(s - m_new)
        l_ref[...] = alpha * l_ref[...] + jnp.sum(p, axis=0, keepdims=True)
        acc_ref[...] = alpha * acc_ref[...] + jnp.dot(
            vt, p.astype(vt.dtype), preferred_element_type=F32)
        m_ref[...] = m_new

    def body(ki, carry):
        step(ki, False)
        return carry

    lax.fori_loop(0, qi, body, 0)
    step(qi, True)

    lam = (jnp.exp(jnp.sum(lq1_ref[...] * lk1_ref[...], axis=-1, keepdims=True))
           - jnp.exp(jnp.sum(lq2_ref[...] * lk2_ref[...], axis=-1, keepdims=True))
           + lam_init)
    o_all = acc_ref[...] / l_ref[...]
    o = o_all[:, 0:tq] - lam * o_all[:, tq:2 * tq]
    ms = jnp.mean(o * o, axis=0, keepdims=True)
    o = o * lax.rsqrt(ms + EPS) * sw_ref[...] * (1.0 - lam_init)
    o_ref[...] = o.T.astype(o_ref.dtype)


def _attention(proj3, vt3, lq1, lk1, lq2, lk2, subln, *, n_heads, dh, lam_init, tq):
    bsz, seq, _ = proj3.shape
    e = 2 * dh
    nk = seq // tq
    assert vt3.shape == (bsz * nk, n_heads * e, tq)
    kern = functools.partial(_attn_kernel, tq=tq, tk=tq, dh=dh, lam_init=lam_init)
    vec = pl.BlockSpec((1, dh), lambda b, h, i: (0, 0))
    return pl.pallas_call(
        kern,
        grid=(bsz, n_heads, nk),
        in_specs=[pl.BlockSpec((None, tq, e), lambda b, h, i: (b, i, h)),
                  pl.BlockSpec((None, seq, e), lambda b, h, i: (b, 0, n_heads + h)),
                  pl.BlockSpec((nk, e, tq), lambda b, h, i: (b, h, 0)),
                  vec, vec, vec, vec,
                  pl.BlockSpec((e, 1), lambda b, h, i: (0, 0))],
        out_specs=pl.BlockSpec((None, tq, e), lambda b, h, i: (b, i, h)),
        out_shape=jax.ShapeDtypeStruct((bsz, seq, n_heads * e), BF16),
        scratch_shapes=[pltpu.VMEM((2 * tq, e), BF16),
                        pltpu.VMEM((1, 2 * tq), F32),
                        pltpu.VMEM((1, 2 * tq), F32),
                        pltpu.VMEM((e, 2 * tq), F32)],
        compiler_params=_params("parallel", "parallel", "arbitrary"),
        name="diff_attention",
    )(proj3, proj3, vt3, lq1, lk1, lq2, lk2, subln)


def _ssm_tables(a_re, a_im, log_dt, b_re, b_im, c_re, c_im, d_skip, *, seg_rows):
    g, p, h = b_re.shape
    tc = SSM_CHUNK
    gpb = LANES // h
    jb = g // gpb
    dt = jnp.exp(log_dt)[:, None]
    er, ei = a_re * dt, a_im * dt

    def powers(n):
        n = jnp.asarray(n, F32).reshape((-1, 1, 1))
        mag = jnp.exp(n * er)
        return mag * jnp.cos(n * ei), mag * jnp.sin(n * ei)

    ar, ai = powers([1.0])
    ar, ai = ar[0], ai[0]
    den = a_re * a_re + a_im * a_im
    fr = ((ar - 1.0) * a_re + ai * a_im) / den
    fi = (ai * a_re - (ar - 1.0) * a_im) / den
    bbr = fr[..., None] * b_re - fi[..., None] * b_im
    bbi = fr[..., None] * b_im + fi[..., None] * b_re

    pr, pi = powers(jnp.arange(tc + 1))
    cpr = c_re[None] * pr[:, :, None, :] - c_im[None] * pi[:, :, None, :]
    cpi = c_re[None] * pi[:, :, None, :] + c_im[None] * pr[:, :, None, :]
    kern = (jnp.einsum('tgop,gpi->tgio', cpr[:tc], bbr)
            - jnp.einsum('tgop,gpi->tgio', cpi[:tc], bbi))
    kidx = jnp.arange(tc)
    tau = kidx[None, :] - kidx[:, None]
    kt = jnp.where((tau >= 0)[:, :, None, None, None], kern[jnp.clip(tau, 0)], 0.0)
    eye = jnp.eye(gpb, dtype=F32)
    kt = kt.reshape(tc, tc, jb, gpb, h, h)
    t_mat = jnp.einsum('kqjgio,gf->jkgiqfo', kt, eye).reshape(jb, tc * LANES, tc * LANES)

    qr, qi_ = pr[tc - 1 - kidx], pi[tc - 1 - kidx]
    sr = qr[..., None] * bbr[None] - qi_[..., None] * bbi[None]
    si = qr[..., None] * bbi[None] + qi_[..., None] * bbr[None]
    st = jnp.stack([sr, si], 0).reshape(2, tc, jb, gpb, p, h)
    w_st = jnp.einsum('ckjgpi,gf->jkgicfp', st, eye).reshape(jb, tc * LANES, 2 * gpb * p)

    ot = jnp.stack([cpr[1:], -cpi[1:]], 0).reshape(2, tc, jb, gpb, h, p)
    w_out = jnp.einsum('cqjgop,gf->jcgpqfo', ot, eye).reshape(jb, 2 * gpb * p, tc * LANES)

    def rows(n):
        xr, xi = powers(n)
        x = jnp.stack([xr, xi], 1).reshape(-1, 2, jb, gpb * p)
        return jnp.transpose(x, (2, 0, 1, 3)).reshape(jb, -1, 2 * gpb * p)

    a_chunk = rows([float(tc)])
    a_seg = rows([float(tc * seg_rows)])
    p_pow = rows(tc * jnp.arange(seg_rows))
    d_row = jnp.tile(d_skip.reshape(jb, 1, LANES), (1, 1, tc))
    return (t_mat.astype(BF16), w_st.astype(BF16), w_out.astype(BF16),
            a_chunk, a_seg, p_pow, d_row)


def _cmul(ar, ai, xr, xi):
    return ar * xr - ai * xi, ar * xi + ai * xr


def _ssm_kernel(u_ref, t_ref, wst_ref, wout_ref, ach_ref, aseg_ref, pp_ref, d_ref,
                y_ref, xst_ref, e_ref, sp_ref, *, seg_rows):
    ncb = xst_ref.shape[0]
    hb = ncb // 2
    blk = lambda c: slice(c * LANES, (c + 1) * LANES)
    u = u_ref[...]
    xst = jnp.dot(u, wst_ref[...], preferred_element_type=F32)
    for c in range(ncb):
        xst_ref[c] = xst[:, blk(c)]

    ach = ach_ref[...]
    a_ch = [jnp.broadcast_to(ach[:, blk(c)], (SCAN_ROWS, LANES)) for c in range(ncb)]

    def cstep(a, e, c):
        return (a[c] * e[c] - a[c + hb] * e[c + hb], a[c] * e[c + hb] + a[c + hb] * e[c])

    def scan_body(i, e):
        row0 = pl.multiple_of(i * SCAN_ROWS, SCAN_ROWS)
        new = [None] * ncb
        for c in range(hb):
            e_ref[c, pl.ds(row0, SCAN_ROWS), :] = e[c]
            e_ref[c + hb, pl.ds(row0, SCAN_ROWS), :] = e[c + hb]
            nr, ni = cstep(a_ch, e, c)
            new[c] = nr + xst_ref.at[c][pl.ds(i, SCAN_ROWS, stride=seg_rows), :]
            new[c + hb] = ni + xst_ref.at[c + hb][pl.ds(i, SCAN_ROWS, stride=seg_rows), :]
        return tuple(new)

    zero = jnp.zeros((SCAN_ROWS, LANES), F32)
    end = lax.fori_loop(0, seg_rows, scan_body, (zero,) * ncb)

    asg = aseg_ref[...]
    a_sg = [asg[:, blk(c)] for c in range(ncb)]
    car = [jnp.zeros((1, LANES), F32)] * ncb
    pp = pp_ref[...]
    p_pw = [pp[:, blk(c)] for c in range(ncb)]
    for s in range(SCAN_ROWS):
        rows = slice(s * seg_rows, (s + 1) * seg_rows)
        nxt = [None] * ncb
        for c in range(hb):
            fr, fi = cstep(p_pw, car, c)
            sp_ref[rows, blk(c)] = (
                e_ref.at[c][pl.ds(s, seg_rows, stride=SCAN_ROWS), :] + fr).astype(sp_ref.dtype)
            sp_ref[rows, blk(c + hb)] = (
                e_ref.at[c + hb][pl.ds(s, seg_rows, stride=SCAN_ROWS), :] + fi).astype(sp_ref.dtype)
            nr, ni = cstep(a_sg, car, c)
            nxt[c] = nr + end[c][s:s + 1, :]
            nxt[c + hb] = ni + end[c + hb][s:s + 1, :]
        car = nxt

    y = jnp.dot(u, t_ref[...], preferred_element_type=F32)
    y = y + jnp.dot(sp_ref[...], wout_ref[...], preferred_element_type=F32)
    y = y + d_ref[...] * u.astype(F32)
    y_ref[...] = y.astype(y_ref.dtype)


def _ssm(u, tables, *, bsz, seg_rows):
    t_mat, w_st, w_out, a_chunk, a_seg, p_pow, d_row = tables
    jb, m, kc = u.shape
    nc = m // bsz
    s2 = w_st.shape[2]
    assert nc == SCAN_ROWS * seg_rows
    kern = functools.partial(_ssm_kernel, seg_rows=seg_rows)
    wspec = lambda r, c: pl.BlockSpec((None, r, c), lambda j, b: (j, 0, 0))
    return pl.pallas_call(
        kern,
        grid=(jb, bsz),
        in_specs=[pl.BlockSpec((None, nc, kc), lambda j, b: (j, b, 0)),
                  wspec(kc, kc), wspec(kc, s2), wspec(s2, kc),
                  wspec(1, s2), wspec(1, s2), wspec(seg_rows, s2), wspec(1, kc)],
        out_specs=pl.BlockSpec((None, nc, kc), lambda j, b: (j, b, 0)),
        out_shape=jax.ShapeDtypeStruct((jb, m, kc), BF16),
        scratch_shapes=[pltpu.VMEM((s2 // LANES, nc, LANES), F32),
                        pltpu.VMEM((s2 // LANES, nc, LANES), F32),
                        pltpu.VMEM((nc, s2), BF16)],
        compiler_params=_params("parallel", "arbitrary"),
        name="s5_scan",
    )(u, t_mat, w_st, w_out, a_chunk, a_seg, p_pow, d_row)


def _merge_kernel(x_ref, ya_ref, ys_ref, ga_ref, gs_ref, wglu_ref, bglu_ref, wa_ref,
                  ws_ref, wo_ref, npost_ref, npre_ref, h_ref, z_ref):
    ys = jax.nn.gelu(ys_ref[...].astype(F32), approximate=True)
    gate = jnp.dot(ys.astype(BF16), wglu_ref[...], preferred_element_type=F32)
    ys = ys * jax.nn.sigmoid(gate + bglu_ref[...])
    ma = jnp.dot(ya_ref[...], wa_ref[...], preferred_element_type=F32)
    ms = jnp.dot(ys.astype(BF16), ws_ref[...], preferred_element_type=F32)
    merged = (jax.nn.sigmoid(ga_ref[...].astype(F32)) * ma
              + jax.nn.sigmoid(gs_ref[...].astype(F32)) * ms)
    mix = jnp.dot(merged.astype(BF16), wo_ref[...], preferred_element_type=F32)
    h = x_ref[...] + _rms(mix, npost_ref[...])
    h_ref[...] = h
    z_ref[...] = _rms(h, npre_ref[...]).astype(z_ref.dtype)


def _merge(x2, y_a, y_s, proj, w_glu, b_glu, w_a, w_s, w_o, n_post, n_pre, *, ga_blk):
    t, d = x2.shape
    wa = y_a.shape[1]
    wsm = y_s.shape[1]
    tm = _tile(t, 256)
    row = lambda c: pl.BlockSpec((tm, c), lambda i: (i, 0))
    full = lambda a: pl.BlockSpec(a.shape, lambda i: (0, 0), pipeline_mode=pl.Buffered(1))
    return pl.pallas_call(
        _merge_kernel,
        grid=(t // tm,),
        in_specs=[row(d), row(wa), row(wsm),
                  pl.BlockSpec((tm, d), lambda i: (i, ga_blk)),
                  pl.BlockSpec((tm, d), lambda i: (i, ga_blk + 1)),
                  full(w_glu), full(b_glu), full(w_a), full(w_s), full(w_o),
                  full(n_post), full(n_pre)],
        out_specs=[row(d), row(d)],
        out_shape=[jax.ShapeDtypeStruct((t, d), F32),
                   jax.ShapeDtypeStruct((t, d), BF16)],
        compiler_params=_params("parallel"),
        name="merge",
    )(x2, y_a, y_s, proj, proj, w_glu, b_glu, w_a, w_s, w_o, n_post, n_pre)


def _ffn_kernel(z_ref, h_ref, wg_ref, wu_ref, wd_ref, npost_ref, o_ref, acc_ref):
    j = pl.program_id(1)
    z = z_ref[...]
    g = jnp.dot(z, wg_ref[...], preferred_element_type=F32)
    u = jnp.dot(z, wu_ref[...], preferred_element_type=F32)
    a = (jax.nn.silu(g) * u).astype(BF16)
    part = jnp.dot(a, wd_ref[...], preferred_element_type=F32)

    @pl.when(j == 0)
    def _():
        acc_ref[...] = part

    @pl.when(j > 0)
    def _():
        acc_ref[...] += part

    @pl.when(j == pl.num_programs(1) - 1)
    def _():
        o_ref[...] = h_ref[...] + _rms(acc_ref[...], npost_ref[...])


def _ffn(z, h, w_g, w_u, w_d, n_post):
    t, d = z.shape
    f = w_g.shape[1]
    tm = _tile(t, 512)
    tf = _tile(f, 512)
    return pl.pallas_call(
        _ffn_kernel,
        grid=(t // tm, f // tf),
        in_specs=[pl.BlockSpec((tm, d), lambda i, j: (i, 0)),
                  pl.BlockSpec((tm, d), lambda i, j: (i, 0)),
                  pl.BlockSpec((d, tf), lambda i, j: (0, j)),
                  pl.BlockSpec((d, tf), lambda i, j: (0, j)),
                  pl.BlockSpec((tf, d), lambda i, j: (j, 0)),
                  pl.BlockSpec((1, d), lambda i, j: (0, 0))],
        out_specs=pl.BlockSpec((tm, d), lambda i, j: (i, 0)),
        out_shape=jax.ShapeDtypeStruct((t, d), F32),
        scratch_shapes=[pltpu.VMEM((tm, d), F32)],
        compiler_params=_params("parallel", "arbitrary"),
        name="ffn",
    )(z, h, w_g, w_u, w_d, n_post)


def _lambda_init(layer_idx):
    return 0.8 - 0.6 * math.exp(-0.3 * layer_idx)


def kernel(x, w_in, lambda_q1, lambda_k1, lambda_q2, lambda_k2, subln_w, ssm_a_re, ssm_a_im, ssm_log_dt, ssm_b_re, ssm_b_im, ssm_c_re, ssm_c_im, ssm_d, w_glu, b_glu, w_attn_branch, w_ssm_branch, w_out, norm_mix_pre, norm_mix_post, w_ffn_gate, w_ffn_up, w_ffn_down, norm_ffn_pre, norm_ffn_post):
    bsz, seq, d = x.shape
    depth = w_in.shape[0]
    dh = lambda_q1.shape[-1]
    e = 2 * dh
    attn_w = w_attn_branch.shape[1]
    ssm_w = w_ssm_branch.shape[1]
    n_heads = attn_w // e
    jb = ssm_w // LANES
    tc = SSM_CHUNK
    nc = seq // tc
    seg_rows = nc // SCAN_ROWS
    t = bsz * seq
    row = lambda v: v.reshape(1, -1).astype(F32)

    tm = _tile(seq, 512)
    qscale = dh ** -0.5 * math.log2(math.e)

    h = x.reshape(t, d)
    for l in range(depth):
        lam_init = _lambda_init(l)
        wq, wk, wv, wsi, wga, wgs = jnp.split(
            w_in[l], [attn_w, 2 * attn_w, 3 * attn_w, 3 * attn_w + ssm_w, 3 * attn_w + ssm_w + d],
            axis=1)
        w_main = jnp.concatenate([wq * qscale, wk, wga, wgs, wsi], axis=1).astype(BF16)
        proj, vt3 = _in_proj(h, row(norm_mix_pre[l]), w_main, wv.T.astype(BF16),
                             tm=tm, tn=attn_w)
        y_a = _attention(proj.reshape(bsz, seq, -1), vt3, row(lambda_q1[l]), row(lambda_k1[l]),
                         row(lambda_q2[l]), row(lambda_k2[l]),
                         subln_w[l].reshape(e, 1).astype(F32),
                         n_heads=n_heads, dh=dh, lam_init=lam_init, tq=tm)
        y_a = y_a.reshape(t, attn_w)

        tables = _ssm_tables(ssm_a_re[l], ssm_a_im[l], ssm_log_dt[l], ssm_b_re[l],
                             ssm_b_im[l], ssm_c_re[l], ssm_c_im[l], ssm_d[l],
                             seg_rows=seg_rows)
        s_in = proj[:, 2 * attn_w + 2 * d:]
        u = s_in.reshape(bsz * nc, tc, jb, LANES).transpose(2, 0, 1, 3)
        u = u.reshape(jb, bsz * nc, tc * LANES)
        y_s = _ssm(u, tables, bsz=bsz, seg_rows=seg_rows)
        y_s = y_s.reshape(jb, bsz * nc, tc, LANES).transpose(1, 2, 0, 3).reshape(t, ssm_w)

        h1, z = _merge(h, y_a, y_s, proj, w_glu[l].astype(BF16), row(b_glu[l]),
                       w_attn_branch[l].astype(BF16), w_ssm_branch[l].astype(BF16),
                       w_out[l].astype(BF16), row(norm_mix_post[l]), row(norm_ffn_pre[l]),
                       ga_blk=2 * attn_w // d)
        h = _ffn(z, h1, w_ffn_gate[l].astype(BF16), w_ffn_up[l].astype(BF16),
                 w_ffn_down[l].astype(BF16), row(norm_ffn_post[l]))
    return h.reshape(bsz, seq, d)
```

```python
import functools
import math

import jax
import jax.numpy as jnp
from jax import lax
from jax.experimental import pallas as pl
from jax.experimental.pallas import tpu as pltpu

EPS = 1e-6
NEG_INF = -1e30
LANES = 128
SSM_CHUNK = 8
SCAN_ROWS = 8
ONES_ROWS = 16
VMEM_LIMIT_BYTES = 56 * 1024 * 1024
F32 = jnp.float32
BF16 = jnp.bfloat16
_NT = (((1,), (1,)), ((), ()))


def _params(*sem):
    return pltpu.CompilerParams(dimension_semantics=sem,
                                vmem_limit_bytes=VMEM_LIMIT_BYTES)


def _rms(x, gain):
    ms = jnp.mean(x * x, axis=-1, keepdims=True)
    return x * lax.rsqrt(ms + EPS) * gain


def _tile(n, pref):
    t = min(n, pref)
    assert n % t == 0, (n, t)
    return t


def _in_proj_kernel(x_ref, g_ref, w_ref, wvt_ref, ws_ref, o_ref, vt_ref, uc_ref,
                    u_ref, s_ref, *, e):
    @pl.when(pl.program_id(1) == 0)
    def _():
        u = _rms(x_ref[...], g_ref[...]).astype(u_ref.dtype)
        u_ref[...] = u
        vt = lax.dot_general(wvt_ref[...], u, _NT, preferred_element_type=F32)
        ea = e + ONES_ROWS
        for hd in range(vt.shape[0] // e):
            vt_ref[hd * ea:hd * ea + e, :] = vt[hd * e:(hd + 1) * e, :].astype(vt_ref.dtype)
            vt_ref[hd * ea + e:(hd + 1) * ea, :] = jnp.ones((ONES_ROWS, vt.shape[1]), vt_ref.dtype)
        s = jnp.dot(u, ws_ref[...], preferred_element_type=F32)
        nrow = uc_ref.shape[1]
        for j in range(s_ref.shape[0]):
            s_ref[j] = s[:, j * LANES:(j + 1) * LANES]
        for j in range(s_ref.shape[0]):
            for k in range(SSM_CHUNK):
                uc_ref[j, :, k * LANES:(k + 1) * LANES] = (
                    s_ref.at[j][pl.ds(k, nrow, stride=SSM_CHUNK), :].astype(uc_ref.dtype))

    o_ref[...] = jnp.dot(u_ref[...], w_ref[...],
                         preferred_element_type=F32).astype(o_ref.dtype)


def _in_proj(x2, gain, w, wvt, w_s, *, tm, tn, e):
    t, d = x2.shape
    n = w.shape[1]
    nva = wvt.shape[0] // e * (e + ONES_ROWS)
    jb = w_s.shape[1] // LANES
    kern = functools.partial(_in_proj_kernel, e=e)
    const = lambda a: pl.BlockSpec(a.shape, lambda i, j: (0, 0), pipeline_mode=pl.Buffered(1))
    return pl.pallas_call(
        kern,
        grid=(t // tm, n // tn),
        in_specs=[pl.BlockSpec((tm, d), lambda i, j: (i, 0)),
                  pl.BlockSpec((1, d), lambda i, j: (0, 0)),
                  pl.BlockSpec((d, tn), lambda i, j: (0, j)),
                  const(wvt), const(w_s)],
        out_specs=[pl.BlockSpec((tm, tn), lambda i, j: (i, j)),
                   pl.BlockSpec((None, nva, tm), lambda i, j: (i, 0, 0)),
                   pl.BlockSpec((jb, tm // SSM_CHUNK, SSM_CHUNK * LANES), lambda i, j: (0, i, 0))],
        out_shape=[jax.ShapeDtypeStruct((t, n), BF16),
                   jax.ShapeDtypeStruct((t // tm, nva, tm), BF16),
                   jax.ShapeDtypeStruct((jb, t // SSM_CHUNK, SSM_CHUNK * LANES), BF16)],
        scratch_shapes=[pltpu.VMEM((tm, d), BF16),
                        pltpu.VMEM((jb, tm, LANES), F32)],
        compiler_params=_params("parallel", "arbitrary"),
        name="in_proj",
    )(x2, gain, w, wvt, w_s)


def _attn_kernel(q_ref, k_ref, vt_ref, lq1_ref, lk1_ref, lq2_ref, lk2_ref, sw_ref,
                 o_ref, qs_ref, m_ref, acc_ref, sa_ref, sb_ref, *, tq, tk, dh, lam_init):
    qi = pl.program_id(2)

    q = q_ref[...]
    lane = lax.broadcasted_iota(jnp.int32, q.shape, 1)
    zero = jnp.zeros_like(q)
    qs_ref[0:tq, :] = jnp.where(lane < dh, q, zero)
    qs_ref[tq:2 * tq, :] = jnp.where(lane >= dh, q, zero)
    m_ref[...] = jnp.full(m_ref.shape, NEG_INF, F32)
    acc_ref[...] = jnp.zeros(acc_ref.shape, F32)

    def scores(ki, s_ref):
        k = k_ref[pl.ds(pl.multiple_of(ki * tk, tk), tk), :]
        s_ref[...] = lax.dot_general(k, qs_ref[...], _NT,
                                     preferred_element_type=F32)

    def consume(ki, s_ref, masked):
        s = s_ref[...]
        if masked:
            kpos = ki * tk + lax.broadcasted_iota(jnp.int32, s.shape, 0)
            col = lax.broadcasted_iota(jnp.int32, s.shape, 1)
            qpos = qi * tq + jnp.where(col >= tq, col - tq, col)
            s = jnp.where(kpos <= qpos, s, NEG_INF)
        m_prev = m_ref[...]
        m_new = jnp.maximum(m_prev, jnp.max(s, axis=0, keepdims=True))
        alpha = jnp.exp2(m_prev - m_new)
        p = jnp.exp2(s - m_new)
        acc_ref[...] = alpha * acc_ref[...] + jnp.dot(
            vt_ref[ki], p.astype(vt_ref.dtype), preferred_element_type=F32)
        m_ref[...] = m_new

    scores(0, sa_ref)

    def pair(j, carry):
        scores(2 * j + 1, sb_ref)
        consume(2 * j, sa_ref, False)
        scores(2 * j + 2, sa_ref)
        consume(2 * j + 1, sb_ref, False)
        return carry

    lax.fori_loop(0, qi // 2, pair, 0)

    @pl.when(qi % 2 == 1)
    def _():
        scores(qi, sb_ref)
        consume(qi - 1, sa_ref, False)
        consume(qi, sb_ref, True)

    @pl.when(qi % 2 == 0)
    def _():
        consume(qi, sa_ref, True)

    lam = (jnp.exp(jnp.sum(lq1_ref[...] * lk1_ref[...], axis=-1, keepdims=True))
           - jnp.exp(jnp.sum(lq2_ref[...] * lk2_ref[...], axis=-1, keepdims=True))
           + lam_init)
    e = 2 * dh
    acc = acc_ref[...]
    o_all = acc[0:e, :] / acc[e:e + 1, :]
    o = o_all[:, 0:tq] - lam * o_all[:, tq:2 * tq]
    ms = jnp.mean(o * o, axis=0, keepdims=True)
    o = o * lax.rsqrt(ms + EPS) * sw_ref[...] * (1.0 - lam_init)
    o_ref[...] = o.T.astype(o_ref.dtype)


def _attention(proj3, vt3, lq1, lk1, lq2, lk2, subln, *, n_heads, dh, lam_init, tq):
    bsz, seq, _ = proj3.shape
    e = 2 * dh
    nk = seq // tq
    ea = e + ONES_ROWS
    assert vt3.shape == (bsz * nk, n_heads * ea, tq)
    kern = functools.partial(_attn_kernel, tq=tq, tk=tq, dh=dh, lam_init=lam_init)
    vec = pl.BlockSpec((1, dh), lambda b, h, i: (0, 0))
    return pl.pallas_call(
        kern,
        grid=(bsz, n_heads, nk),
        in_specs=[pl.BlockSpec((None, tq, e), lambda b, h, i: (b, i, h)),
                  pl.BlockSpec((None, seq, e), lambda b, h, i: (b, 0, n_heads + h)),
                  pl.BlockSpec((nk, ea, tq), lambda b, h, i: (b, h, 0)),
                  vec, vec, vec, vec,
                  pl.BlockSpec((e, 1), lambda b, h, i: (0, 0))],
        out_specs=pl.BlockSpec((None, tq, e), lambda b, h, i: (b, i, h)),
        out_shape=jax.ShapeDtypeStruct((bsz, seq, n_heads * e), BF16),
        scratch_shapes=[pltpu.VMEM((2 * tq, e), BF16),
                        pltpu.VMEM((1, 2 * tq), F32),
                        pltpu.VMEM((ea, 2 * tq), F32),
                        pltpu.VMEM((tq, 2 * tq), F32),
                        pltpu.VMEM((tq, 2 * tq), F32)],
        compiler_params=_params("parallel", "parallel", "arbitrary"),
        name="diff_attention",
    )(proj3, proj3, vt3, lq1, lk1, lq2, lk2, subln)


def _ssm_tables(a_re, a_im, log_dt, b_re, b_im, c_re, c_im, d_skip, *, seg_rows):
    g, p, h = b_re.shape
    tc = SSM_CHUNK
    gpb = LANES // h
    jb = g // gpb
    dt = jnp.exp(log_dt)[:, None]
    er, ei = a_re * dt, a_im * dt

    def powers(n):
        n = jnp.asarray(n, F32).reshape((-1, 1, 1))
        mag = jnp.exp(n * er)
        return mag * jnp.cos(n * ei), mag * jnp.sin(n * ei)

    ar, ai = powers([1.0])
    ar, ai = ar[0], ai[0]
    den = a_re * a_re + a_im * a_im
    fr = ((ar - 1.0) * a_re + ai * a_im) / den
    fi = (ai * a_re - (ar - 1.0) * a_im) / den
    bbr = fr[..., None] * b_re - fi[..., None] * b_im
    bbi = fr[..., None] * b_im + fi[..., None] * b_re

    pr, pi = powers(jnp.arange(tc + 1))
    cpr = c_re[None] * pr[:, :, None, :] - c_im[None] * pi[:, :, None, :]
    cpi = c_re[None] * pi[:, :, None, :] + c_im[None] * pr[:, :, None, :]
    kern = (jnp.einsum('tgop,gpi->tgio', cpr[:tc], bbr)
            - jnp.einsum('tgop,gpi->tgio', cpi[:tc], bbi))
    kidx = jnp.arange(tc)
    tau = kidx[None, :] - kidx[:, None]
    kt = jnp.where((tau >= 0)[:, :, None, None, None], kern[jnp.clip(tau, 0)], 0.0)
    eye = jnp.eye(gpb, dtype=F32)
    kt = kt.reshape(tc, tc, jb, gpb, h, h)
    t_mat = jnp.einsum('kqjgio,gf->jkgiqfo', kt, eye).reshape(jb, tc * LANES, tc * LANES)

    qr, qi_ = pr[tc - 1 - kidx], pi[tc - 1 - kidx]
    sr = qr[..., None] * bbr[None] - qi_[..., None] * bbi[None]
    si = qr[..., None] * bbi[None] + qi_[..., None] * bbr[None]
    st = jnp.stack([sr, si], 0).reshape(2, tc, jb, gpb, p, h)
    w_st = jnp.einsum('ckjgpi,gf->jkgicfp', st, eye).reshape(jb, tc * LANES, 2 * gpb * p)

    ot = jnp.stack([cpr[1:], -cpi[1:]], 0).reshape(2, tc, jb, gpb, h, p)
    w_out = jnp.einsum('cqjgop,gf->jcgpqfo', ot, eye).reshape(jb, 2 * gpb * p, tc * LANES)

    def rows(n):
        xr, xi = powers(n)
        x = jnp.stack([xr, xi], 1).reshape(-1, 2, jb, gpb * p)
        return jnp.transpose(x, (2, 0, 1, 3)).reshape(jb, -1, 2 * gpb * p)

    a_chunk = rows([float(tc)])
    a_seg = rows([float(tc * seg_rows)])
    p_pow = rows(tc * jnp.arange(seg_rows))
    d_row = jnp.tile(d_skip.reshape(jb, 1, LANES), (1, 1, tc))
    return (t_mat.astype(BF16), w_st.astype(BF16), w_out.astype(BF16),
            a_chunk, a_seg, p_pow, d_row)


def _cmul(ar, ai, xr, xi):
    return ar * xr - ai * xi, ar * xi + ai * xr


def _ssm_kernel(u_ref, t_ref, wst_ref, wout_ref, ach_ref, aseg_ref, pp_ref, d_ref,
                y_ref, xst_ref, e_ref, sp_ref, *, seg_rows):
    ncb = xst_ref.shape[0]
    hb = ncb // 2
    blk = lambda c: slice(c * LANES, (c + 1) * LANES)
    u = u_ref[...]
    xst = jnp.dot(u, wst_ref[...], preferred_element_type=F32)
    for c in range(ncb):
        xst_ref[c] = xst[:, blk(c)]

    ach = ach_ref[...]
    a_ch = [jnp.broadcast_to(ach[:, blk(c)], (SCAN_ROWS, LANES)) for c in range(ncb)]

    def cstep(a, e, c):
        return (a[c] * e[c] - a[c + hb] * e[c + hb], a[c] * e[c + hb] + a[c + hb] * e[c])

    def scan_body(i, e):
        row0 = pl.multiple_of(i * SCAN_ROWS, SCAN_ROWS)
        new = [None] * ncb
        for c in range(hb):
            e_ref[c, pl.ds(row0, SCAN_ROWS), :] = e[c]
            e_ref[c + hb, pl.ds(row0, SCAN_ROWS), :] = e[c + hb]
            nr, ni = cstep(a_ch, e, c)
            new[c] = nr + xst_ref.at[c][pl.ds(i, SCAN_ROWS, stride=seg_rows), :]
            new[c + hb] = ni + xst_ref.at[c + hb][pl.ds(i, SCAN_ROWS, stride=seg_rows), :]
        return tuple(new)

    zero = jnp.zeros((SCAN_ROWS, LANES), F32)
    end = lax.fori_loop(0, seg_rows, scan_body, (zero,) * ncb)

    asg = aseg_ref[...]
    a_sg = [asg[:, blk(c)] for c in range(ncb)]
    car = [jnp.zeros((1, LANES), F32)] * ncb
    pp = pp_ref[...]
    p_pw = [pp[:, blk(c)] for c in range(ncb)]
    for s in range(SCAN_ROWS):
        rows = slice(s * seg_rows, (s + 1) * seg_rows)
        nxt = [None] * ncb
        for c in range(hb):
            fr, fi = cstep(p_pw, car, c)
            sp_ref[rows, blk(c)] = (
                e_ref.at[c][pl.ds(s, seg_rows, stride=SCAN_ROWS), :] + fr).astype(sp_ref.dtype)
            sp_ref[rows, blk(c + hb)] = (
                e_ref.at[c + hb][pl.ds(s, seg_rows, stride=SCAN_ROWS), :] + fi).astype(sp_ref.dtype)
            nr, ni = cstep(a_sg, car, c)
            nxt[c] = nr + end[c][s:s + 1, :]
            nxt[c + hb] = ni + end[c + hb][s:s + 1, :]
        car = nxt

    y = jnp.dot(u, t_ref[...], preferred_element_type=F32)
    y = y + jnp.dot(sp_ref[...], wout_ref[...], preferred_element_type=F32)
    y = y + d_ref[...] * u.astype(F32)
    y_ref[...] = y.astype(y_ref.dtype)


def _ssm(u, tables, *, bsz, seg_rows):
    t_mat, w_st, w_out, a_chunk, a_seg, p_pow, d_row = tables
    jb, m, kc = u.shape
    nc = m // bsz
    s2 = w_st.shape[2]
    assert nc == SCAN_ROWS * seg_rows
    kern = functools.partial(_ssm_kernel, seg_rows=seg_rows)
    wspec = lambda r, c: pl.BlockSpec((None, r, c), lambda j, b: (j, 0, 0))
    return pl.pallas_call(
        kern,
        grid=(jb, bsz),
        in_specs=[pl.BlockSpec((None, nc, kc), lambda j, b: (j, b, 0)),
                  wspec(kc, kc), wspec(kc, s2), wspec(s2, kc),
                  wspec(1, s2), wspec(1, s2), wspec(seg_rows, s2), wspec(1, kc)],
        out_specs=pl.BlockSpec((None, nc, kc), lambda j, b: (j, b, 0)),
        out_shape=jax.ShapeDtypeStruct((jb, m, kc), BF16),
        scratch_shapes=[pltpu.VMEM((s2 // LANES, nc, LANES), F32),
                        pltpu.VMEM((s2 // LANES, nc, LANES), F32),
                        pltpu.VMEM((nc, s2), BF16)],
        compiler_params=_params("parallel", "arbitrary"),
        name="s5_scan",
    )(u, t_mat, w_st, w_out, a_chunk, a_seg, p_pow, d_row)


def _merge_kernel(x_ref, ya_ref, ys_ref, ga_ref, gs_ref, wglu_ref, bglu_ref, wa_ref,
                  ws_ref, wo_ref, npost_ref, npre_ref, h_ref, z_ref):
    ys = jax.nn.gelu(ys_ref[...].astype(F32), approximate=True)
    gate = jnp.dot(ys.astype(BF16), wglu_ref[...], preferred_element_type=F32)
    ys = ys * jax.nn.sigmoid(gate + bglu_ref[...])
    ma = jnp.dot(ya_ref[...], wa_ref[...], preferred_element_type=F32)
    ms = jnp.dot(ys.astype(BF16), ws_ref[...], preferred_element_type=F32)
    merged = (jax.nn.sigmoid(ga_ref[...].astype(F32)) * ma
              + jax.nn.sigmoid(gs_ref[...].astype(F32)) * ms)
    mix = jnp.dot(merged.astype(BF16), wo_ref[...], preferred_element_type=F32)
    h = x_ref[...] + _rms(mix, npost_ref[...])
    h_ref[...] = h
    z_ref[...] = _rms(h, npre_ref[...]).astype(z_ref.dtype)


def _merge(x2, y_a, y_s, proj, w_glu, b_glu, w_a, w_s, w_o, n_post, n_pre, *, ga_blk):
    t, d = x2.shape
    wa = y_a.shape[1]
    wsm = y_s.shape[1]
    tm = _tile(t, 256)
    row = lambda c: pl.BlockSpec((tm, c), lambda i: (i, 0))
    full = lambda a: pl.BlockSpec(a.shape, lambda i: (0, 0), pipeline_mode=pl.Buffered(1))
    return pl.pallas_call(
        _merge_kernel,
        grid=(t // tm,),
        in_specs=[row(d), row(wa), row(wsm),
                  pl.BlockSpec((tm, d), lambda i: (i, ga_blk)),
                  pl.BlockSpec((tm, d), lambda i: (i, ga_blk + 1)),
                  full(w_glu), full(b_glu), full(w_a), full(w_s), full(w_o),
                  full(n_post), full(n_pre)],
        out_specs=[row(d), row(d)],
        out_shape=[jax.ShapeDtypeStruct((t, d), F32),
                   jax.ShapeDtypeStruct((t, d), BF16)],
        compiler_params=_params("parallel"),
        name="merge",
    )(x2, y_a, y_s, proj, proj, w_glu, b_glu, w_a, w_s, w_o, n_post, n_pre)


def _ffn_kernel(z_ref, h_ref, wg_ref, wu_ref, wd_ref, npost_ref, o_ref):
    j = pl.program_id(1)

    @pl.when(j == 0)
    def _():
        o_ref[...] = jnp.zeros(o_ref.shape, o_ref.dtype)

    z = z_ref[...]
    g = jnp.dot(z, wg_ref[...], preferred_element_type=F32)
    u = jnp.dot(z, wu_ref[...], preferred_element_type=F32)
    a = (jax.nn.silu(g) * u).astype(BF16)
    o_ref[...] += jnp.dot(a, wd_ref[...], preferred_element_type=F32)

    @pl.when(j == pl.num_programs(1) - 1)
    def _():
        o_ref[...] = h_ref[...] + _rms(o_ref[...], npost_ref[...])


def _ffn(z, h, w_g, w_u, w_d, n_post):
    t, d = z.shape
    f = w_g.shape[1]
    tm = _tile(t, 512)
    tf = _tile(f, 512)
    return pl.pallas_call(
        _ffn_kernel,
        grid=(t // tm, f // tf),
        in_specs=[pl.BlockSpec((tm, d), lambda i, j: (i, 0)),
                  pl.BlockSpec((tm, d), lambda i, j: (i, 0)),
                  pl.BlockSpec((d, tf), lambda i, j: (0, j)),
                  pl.BlockSpec((d, tf), lambda i, j: (0, j)),
                  pl.BlockSpec((tf, d), lambda i, j: (j, 0)),
                  pl.BlockSpec((1, d), lambda i, j: (0, 0))],
        out_specs=pl.BlockSpec((tm, d), lambda i, j: (i, 0)),
        out_shape=jax.ShapeDtypeStruct((t, d), F32),
        compiler_params=_params("parallel", "arbitrary"),
        name="ffn",
    )(z, h, w_g, w_u, w_d, n_post)


def _lambda_init(layer_idx):
    return 0.8 - 0.6 * math.exp(-0.3 * layer_idx)


def kernel(x, w_in, lambda_q1, lambda_k1, lambda_q2, lambda_k2, subln_w, ssm_a_re, ssm_a_im, ssm_log_dt, ssm_b_re, ssm_b_im, ssm_c_re, ssm_c_im, ssm_d, w_glu, b_glu, w_attn_branch, w_ssm_branch, w_out, norm_mix_pre, norm_mix_post, w_ffn_gate, w_ffn_up, w_ffn_down, norm_ffn_pre, norm_ffn_post):
    bsz, seq, d = x.shape
    depth = w_in.shape[0]
    dh = lambda_q1.shape[-1]
    e = 2 * dh
    attn_w = w_attn_branch.shape[1]
    ssm_w = w_ssm_branch.shape[1]
    n_heads = attn_w // e
    jb = ssm_w // LANES
    tc = SSM_CHUNK
    nc = seq // tc
    seg_rows = nc // SCAN_ROWS
    t = bsz * seq
    row = lambda v: v.reshape(1, -1).astype(F32)

    tm = _tile(seq, 512)
    qscale = dh ** -0.5 * math.log2(math.e)

    h = x.reshape(t, d)
    for l in range(depth):
        lam_init = _lambda_init(l)
        wq, wk, wv, wsi, wga, wgs = jnp.split(
            w_in[l], [attn_w, 2 * attn_w, 3 * attn_w, 3 * attn_w + ssm_w, 3 * attn_w + ssm_w + d],
            axis=1)
        w_main = jnp.concatenate([wq * qscale, wk, wga, wgs], axis=1).astype(BF16)
        proj, vt3, u = _in_proj(h, row(norm_mix_pre[l]), w_main, wv.T.astype(BF16),
                                wsi.astype(BF16), tm=tm, tn=d, e=e)
        y_a = _attention(proj.reshape(bsz, seq, -1), vt3, row(lambda_q1[l]), row(lambda_k1[l]),
                         row(lambda_q2[l]), row(lambda_k2[l]),
                         subln_w[l].reshape(e, 1).astype(F32),
                         n_heads=n_heads, dh=dh, lam_init=lam_init, tq=tm)
        y_a = y_a.reshape(t, attn_w)

        tables = _ssm_tables(ssm_a_re[l], ssm_a_im[l], ssm_log_dt[l], ssm_b_re[l],
                             ssm_b_im[l], ssm_c_re[l], ssm_c_im[l], ssm_d[l],
                             seg_rows=seg_rows)
        y_s = _ssm(u, tables, bsz=bsz, seg_rows=seg_rows)
        y_s = y_s.reshape(jb, bsz * nc, tc, LANES).transpose(1, 2, 0, 3).reshape(t, ssm_w)

        h1, z = _merge(h, y_a, y_s, proj, w_glu[l].astype(BF16), row(b_glu[l]),
                       w_attn_branch[l].astype(BF16), w_ssm_branch[l].astype(BF16),
                       w_out[l].astype(BF16), row(norm_mix_post[l]), row(norm_ffn_pre[l]),
                       ga_blk=2 * attn_w // d)
        h = _ffn(z, h1, w_ffn_gate[l].astype(BF16), w_ffn_up[l].astype(BF16),
                 w_ffn_down[l].astype(BF16), row(norm_ffn_post[l]))
    return h.reshape(bsz, seq, d)
```

```python
import functools
import math

import jax
import jax.numpy as jnp
from jax import lax
from jax.experimental import pallas as pl
from jax.experimental.pallas import tpu as pltpu

EPS = 1e-6
NEG_INF = -1e30
LANES = 128
SSM_CHUNK = 8
SCAN_ROWS = 8
ONES_ROWS = 16
VMEM_LIMIT_BYTES = 56 * 1024 * 1024
F32 = jnp.float32
BF16 = jnp.bfloat16
_NT = (((1,), (1,)), ((), ()))


def _params(*sem):
    return pltpu.CompilerParams(dimension_semantics=sem,
                                vmem_limit_bytes=VMEM_LIMIT_BYTES)


def _rms(x, gain):
    ms = jnp.mean(x * x, axis=-1, keepdims=True)
    return x * lax.rsqrt(ms + EPS) * gain


def _tile(n, pref):
    t = min(n, pref)
    assert n % t == 0, (n, t)
    return t


def _in_proj_kernel(x_ref, g_ref, w_ref, wvt_ref, ws_ref, o_ref, vt_ref, uc_ref,
                    u_ref, s_ref, *, e):
    @pl.when(pl.program_id(1) == 0)
    def _():
        u = _rms(x_ref[...], g_ref[...]).astype(u_ref.dtype)
        u_ref[...] = u
        vt = lax.dot_general(wvt_ref[...], u, _NT, preferred_element_type=F32)
        ea = e + ONES_ROWS
        for hd in range(vt.shape[0] // e):
            vt_ref[hd * ea:hd * ea + e, :] = vt[hd * e:(hd + 1) * e, :].astype(vt_ref.dtype)
            vt_ref[hd * ea + e:(hd + 1) * ea, :] = jnp.ones((ONES_ROWS, vt.shape[1]), vt_ref.dtype)
        s = jnp.dot(u, ws_ref[...], preferred_element_type=F32)
        nrow = uc_ref.shape[1]
        for j in range(s_ref.shape[0]):
            s_ref[j] = s[:, j * LANES:(j + 1) * LANES]
        for j in range(s_ref.shape[0]):
            for k in range(SSM_CHUNK):
                uc_ref[j, :, k * LANES:(k + 1) * LANES] = (
                    s_ref.at[j][pl.ds(k, nrow, stride=SSM_CHUNK), :].astype(uc_ref.dtype))

    o_ref[...] = jnp.dot(u_ref[...], w_ref[...],
                         preferred_element_type=F32).astype(o_ref.dtype)


def _in_proj(x2, gain, w, wvt, w_s, *, tm, tn, e):
    t, d = x2.shape
    n = w.shape[1]
    nva = wvt.shape[0] // e * (e + ONES_ROWS)
    jb = w_s.shape[1] // LANES
    kern = functools.partial(_in_proj_kernel, e=e)
    const = lambda a: pl.BlockSpec(a.shape, lambda i, j: (0, 0), pipeline_mode=pl.Buffered(1))
    return pl.pallas_call(
        kern,
        grid=(t // tm, n // tn),
        in_specs=[pl.BlockSpec((tm, d), lambda i, j: (i, 0)),
                  pl.BlockSpec((1, d), lambda i, j: (0, 0)),
                  pl.BlockSpec((d, tn), lambda i, j: (0, j)),
                  const(wvt), const(w_s)],
        out_specs=[pl.BlockSpec((tm, tn), lambda i, j: (i, j)),
                   pl.BlockSpec((None, nva, tm), lambda i, j: (i, 0, 0)),
                   pl.BlockSpec((jb, tm // SSM_CHUNK, SSM_CHUNK * LANES), lambda i, j: (0, i, 0))],
        out_shape=[jax.ShapeDtypeStruct((t, n), BF16),
                   jax.ShapeDtypeStruct((t // tm, nva, tm), BF16),
                   jax.ShapeDtypeStruct((jb, t // SSM_CHUNK, SSM_CHUNK * LANES), BF16)],
        scratch_shapes=[pltpu.VMEM((tm, d), BF16),
                        pltpu.VMEM((jb, tm, LANES), F32)],
        compiler_params=_params("parallel", "arbitrary"),
        name="in_proj",
    )(x2, gain, w, wvt, w_s)


def _attn_kernel(q_ref, qn_ref, k_ref, vt_ref, lq1_ref, lk1_ref, lq2_ref, lk2_ref, sw_ref,
                 o_ref, qs_ref, qns_ref, m_ref, acc_ref, sa_ref, sb_ref, sc_ref,
                 *, tq, tk, dh, lam_init):
    qi = pl.program_id(2)

    def stack(src_ref, dst_ref):
        q = src_ref[...]
        lane = lax.broadcasted_iota(jnp.int32, q.shape, 1)
        zero = jnp.zeros_like(q)
        dst_ref[0:tq, :] = jnp.where(lane < dh, q, zero)
        dst_ref[tq:2 * tq, :] = jnp.where(lane >= dh, q, zero)

    stack(q_ref, qs_ref)
    stack(qn_ref, qns_ref)
    m_ref[...] = jnp.full(m_ref.shape, NEG_INF, F32)
    acc_ref[...] = jnp.zeros(acc_ref.shape, F32)

    def scores(ki, s_ref, stacked_q_ref=qs_ref):
        k = k_ref[pl.ds(pl.multiple_of(ki * tk, tk), tk), :]
        s_ref[...] = lax.dot_general(k, stacked_q_ref[...], _NT,
                                     preferred_element_type=F32)

    def scores_next():
        scores(0, sc_ref, qns_ref)

    def consume(ki, s_ref, masked):
        s = s_ref[...]
        if masked:
            kpos = ki * tk + lax.broadcasted_iota(jnp.int32, s.shape, 0)
            col = lax.broadcasted_iota(jnp.int32, s.shape, 1)
            qpos = qi * tq + jnp.where(col >= tq, col - tq, col)
            s = jnp.where(kpos <= qpos, s, NEG_INF)
        m_prev = m_ref[...]
        m_new = jnp.maximum(m_prev, jnp.max(s, axis=0, keepdims=True))
        alpha = jnp.exp2(m_prev - m_new)
        p = jnp.exp2(s - m_new)
        acc_ref[...] = alpha * acc_ref[...] + jnp.dot(
            vt_ref[ki], p.astype(vt_ref.dtype), preferred_element_type=F32)
        m_ref[...] = m_new

    @pl.when(qi == 0)
    def _():
        scores(0, sa_ref)
        scores_next()
        consume(0, sa_ref, True)

    @pl.when(qi > 0)
    def _():
        scores(1, sb_ref)
        consume(0, sc_ref, False)

        def pair(j, carry):
            scores(2 * j + 2, sa_ref)
            consume(2 * j + 1, sb_ref, False)
            scores(2 * j + 3, sb_ref)
            consume(2 * j + 2, sa_ref, False)
            return carry

        lax.fori_loop(0, (qi - 1) // 2, pair, 0)

        @pl.when(qi % 2 == 1)
        def _():
            scores_next()
            consume(qi, sb_ref, True)

        @pl.when(qi % 2 == 0)
        def _():
            scores(qi, sa_ref)
            consume(qi - 1, sb_ref, False)
            scores_next()
            consume(qi, sa_ref, True)

    lam = (jnp.exp(jnp.sum(lq1_ref[...] * lk1_ref[...], axis=-1, keepdims=True))
           - jnp.exp(jnp.sum(lq2_ref[...] * lk2_ref[...], axis=-1, keepdims=True))
           + lam_init)
    e = 2 * dh
    acc = acc_ref[...]
    o_all = acc[0:e, :] / acc[e:e + 1, :]
    o = o_all[:, 0:tq] - lam * o_all[:, tq:2 * tq]
    ms = jnp.mean(o * o, axis=0, keepdims=True)
    o = o * lax.rsqrt(ms + EPS) * sw_ref[...] * (1.0 - lam_init)
    o_ref[...] = o.T.astype(o_ref.dtype)


def _attention(proj3, vt3, lq1, lk1, lq2, lk2, subln, *, n_heads, dh, lam_init, tq):
    bsz, seq, _ = proj3.shape
    e = 2 * dh
    nk = seq // tq
    ea = e + ONES_ROWS
    assert vt3.shape == (bsz * nk, n_heads * ea, tq)
    kern = functools.partial(_attn_kernel, tq=tq, tk=tq, dh=dh, lam_init=lam_init)
    vec = pl.BlockSpec((1, dh), lambda b, h, i: (0, 0))
    return pl.pallas_call(
        kern,
        grid=(bsz, n_heads, nk),
        in_specs=[pl.BlockSpec((None, tq, e), lambda b, h, i: (b, i, h)),
                  pl.BlockSpec((None, tq, e), lambda b, h, i: (b, jnp.minimum(i + 1, nk - 1), h)),
                  pl.BlockSpec((None, seq, e), lambda b, h, i: (b, 0, n_heads + h)),
                  pl.BlockSpec((nk, ea, tq), lambda b, h, i: (b, h, 0)),
                  vec, vec, vec, vec,
                  pl.BlockSpec((e, 1), lambda b, h, i: (0, 0))],
        out_specs=pl.BlockSpec((None, tq, e), lambda b, h, i: (b, i, h)),
        out_shape=jax.ShapeDtypeStruct((bsz, seq, n_heads * e), BF16),
        scratch_shapes=[pltpu.VMEM((2 * tq, e), BF16),
                        pltpu.VMEM((2 * tq, e), BF16),
                        pltpu.VMEM((1, 2 * tq), F32),
                        pltpu.VMEM((ea, 2 * tq), F32),
                        pltpu.VMEM((tq, 2 * tq), F32),
                        pltpu.VMEM((tq, 2 * tq), F32),
                        pltpu.VMEM((tq, 2 * tq), F32)],
        compiler_params=_params("parallel", "parallel", "arbitrary"),
        name="diff_attention",
    )(proj3, proj3, proj3, vt3, lq1, lk1, lq2, lk2, subln)


def _ssm_tables(a_re, a_im, log_dt, b_re, b_im, c_re, c_im, d_skip, *, seg_rows):
    g, p, h = b_re.shape
    tc = SSM_CHUNK
    gpb = LANES // h
    jb = g // gpb
    dt = jnp.exp(log_dt)[:, None]
    er, ei = a_re * dt, a_im * dt

    def powers(n):
        n = jnp.asarray(n, F32).reshape((-1, 1, 1))
        mag = jnp.exp(n * er)
        return mag * jnp.cos(n * ei), mag * jnp.sin(n * ei)

    ar, ai = powers([1.0])
    ar, ai = ar[0], ai[0]
    den = a_re * a_re + a_im * a_im
    fr = ((ar - 1.0) * a_re + ai * a_im) / den
    fi = (ai * a_re - (ar - 1.0) * a_im) / den
    bbr = fr[..., None] * b_re - fi[..., None] * b_im
    bbi = fr[..., None] * b_im + fi[..., None] * b_re

    pr, pi = powers(jnp.arange(tc + 1))
    cpr = c_re[None] * pr[:, :, None, :] - c_im[None] * pi[:, :, None, :]
    cpi = c_re[None] * pi[:, :, None, :] + c_im[None] * pr[:, :, None, :]
    kern = (jnp.einsum('tgop,gpi->tgio', cpr[:tc], bbr)
            - jnp.einsum('tgop,gpi->tgio', cpi[:tc], bbi))
    kidx = jnp.arange(tc)
    tau = kidx[None, :] - kidx[:, None]
    kt = jnp.where((tau >= 0)[:, :, None, None, None], kern[jnp.clip(tau, 0)], 0.0)
    eye = jnp.eye(gpb, dtype=F32)
    kt = kt.reshape(tc, tc, jb, gpb, h, h)
    t_mat = jnp.einsum('kqjgio,gf->jkgiqfo', kt, eye).reshape(jb, tc * LANES, tc * LANES)

    qr, qi_ = pr[tc - 1 - kidx], pi[tc - 1 - kidx]
    sr = qr[..., None] * bbr[None] - qi_[..., None] * bbi[None]
    si = qr[..., None] * bbi[None] + qi_[..., None] * bbr[None]
    st = jnp.stack([sr, si], 0).reshape(2, tc, jb, gpb, p, h)
    w_st = jnp.einsum('ckjgpi,gf->jkgicfp', st, eye).reshape(jb, tc * LANES, 2 * gpb * p)

    ot = jnp.stack([cpr[1:], -cpi[1:]], 0).reshape(2, tc, jb, gpb, h, p)
    w_out = jnp.einsum('cqjgop,gf->jcgpqfo', ot, eye).reshape(jb, 2 * gpb * p, tc * LANES)

    def rows(n):
        xr, xi = powers(n)
        x = jnp.stack([xr, xi], 1).reshape(-1, 2, jb, gpb * p)
        return jnp.transpose(x, (2, 0, 1, 3)).reshape(jb, -1, 2 * gpb * p)

    a_chunk = rows([float(tc)])
    a_seg = rows([float(tc * seg_rows)])
    p_pow = rows(tc * jnp.arange(seg_rows))
    d_row = jnp.tile(d_skip.reshape(jb, 1, LANES), (1, 1, tc))
    return (t_mat.astype(BF16), w_st.astype(BF16), w_out.astype(BF16),
            a_chunk, a_seg, p_pow, d_row)


def _cmul(ar, ai, xr, xi):
    return ar * xr - ai * xi, ar * xi + ai * xr


def _ssm_kernel(u_ref, t_ref, wst_ref, wout_ref, ach_ref, aseg_ref, pp_ref, d_ref,
                y_ref, xst_ref, e_ref, sp_ref, z_ref, *, seg_rows):
    ncb = xst_ref.shape[0]
    hb = ncb // 2
    blk = lambda c: slice(c * LANES, (c + 1) * LANES)
    u = u_ref[...]
    xst = jnp.dot(u, wst_ref[...], preferred_element_type=F32)
    for c in range(ncb):
        xst_ref[c] = xst[:, blk(c)]

    ach = ach_ref[...]
    a_ch = [jnp.broadcast_to(ach[:, blk(c)], (SCAN_ROWS, LANES)) for c in range(ncb)]

    def cstep(a, e, c):
        return (a[c] * e[c] - a[c + hb] * e[c + hb], a[c] * e[c + hb] + a[c + hb] * e[c])

    def scan_body(i, e):
        row0 = pl.multiple_of(i * SCAN_ROWS, SCAN_ROWS)
        new = [None] * ncb
        for c in range(hb):
            e_ref[c, pl.ds(row0, SCAN_ROWS), :] = e[c]
            e_ref[c + hb, pl.ds(row0, SCAN_ROWS), :] = e[c + hb]
            nr, ni = cstep(a_ch, e, c)
            new[c] = nr + xst_ref.at[c][pl.ds(i, SCAN_ROWS, stride=seg_rows), :]
            new[c + hb] = ni + xst_ref.at[c + hb][pl.ds(i, SCAN_ROWS, stride=seg_rows), :]
        return tuple(new)

    zero = jnp.zeros((SCAN_ROWS, LANES), F32)
    end = lax.fori_loop(0, seg_rows, scan_body, (zero,) * ncb, unroll=4)

    asg = aseg_ref[...]
    a_sg = [asg[:, blk(c)] for c in range(ncb)]
    car = [jnp.zeros((1, LANES), F32)] * ncb
    pp = pp_ref[...]
    p_pw = [pp[:, blk(c)] for c in range(ncb)]
    for s in range(SCAN_ROWS):
        rows = slice(s * seg_rows, (s + 1) * seg_rows)
        nxt = [None] * ncb
        for c in range(hb):
            fr, fi = cstep(p_pw, car, c)
            sp_ref[rows, blk(c)] = (
                e_ref.at[c][pl.ds(s, seg_rows, stride=SCAN_ROWS), :] + fr).astype(sp_ref.dtype)
            sp_ref[rows, blk(c + hb)] = (
                e_ref.at[c + hb][pl.ds(s, seg_rows, stride=SCAN_ROWS), :] + fi).astype(sp_ref.dtype)
            nr, ni = cstep(a_sg, car, c)
            nxt[c] = nr + end[c][s:s + 1, :]
            nxt[c + hb] = ni + end[c + hb][s:s + 1, :]
        car = nxt

    y = jnp.dot(u, t_ref[...], preferred_element_type=F32)
    y = y + jnp.dot(sp_ref[...], wout_ref[...], preferred_element_type=F32)
    y = y + d_ref[...] * u.astype(F32)

    tc = SSM_CHUNK
    ng = y.shape[0] // SCAN_ROWS
    for k in range(tc):
        z_ref[:, k * SCAN_ROWS:(k + 1) * SCAN_ROWS, :] = (
            y[:, blk(k)].reshape(ng, SCAN_ROWS, LANES))
    for c in range(SCAN_ROWS):
        y_ref[:, c * tc:(c + 1) * tc, :] = z_ref[:, pl.ds(c, tc, stride=SCAN_ROWS), :]


def _ssm(u, tables, *, bsz, seg_rows):
    t_mat, w_st, w_out, a_chunk, a_seg, p_pow, d_row = tables
    jb, m, kc = u.shape
    nc = m // bsz
    s2 = w_st.shape[2]
    assert nc == SCAN_ROWS * seg_rows
    kern = functools.partial(_ssm_kernel, seg_rows=seg_rows)
    wspec = lambda r, c: pl.BlockSpec((None, r, c), lambda j, b: (j, 0, 0))
    return pl.pallas_call(
        kern,
        grid=(jb, bsz),
        in_specs=[pl.BlockSpec((None, nc, kc), lambda j, b: (j, b, 0)),
                  wspec(kc, kc), wspec(kc, s2), wspec(s2, kc),
                  wspec(1, s2), wspec(1, s2), wspec(seg_rows, s2), wspec(1, kc)],
        out_specs=pl.BlockSpec((nc // SCAN_ROWS, SCAN_ROWS * SSM_CHUNK, LANES),
                               lambda j, b: (b, 0, j)),
        out_shape=jax.ShapeDtypeStruct((m // SCAN_ROWS, SCAN_ROWS * SSM_CHUNK, jb * LANES), F32),
        scratch_shapes=[pltpu.VMEM((s2 // LANES, nc, LANES), F32),
                        pltpu.VMEM((s2 // LANES, nc, LANES), F32),
                        pltpu.VMEM((nc, s2), BF16),
                        pltpu.VMEM((nc // SCAN_ROWS, SCAN_ROWS * SSM_CHUNK, LANES), F32)],
        compiler_params=_params("parallel", "arbitrary"),
        name="s5_scan",
    )(u, t_mat, w_st, w_out, a_chunk, a_seg, p_pow, d_row)


def _merge_kernel(x_ref, ya_ref, ys_ref, ga_ref, gs_ref, wglu_ref, bglu_ref, wa_ref,
                  ws_ref, wo_ref, npost_ref, npre_ref, h_ref, z_ref):
    ys = jax.nn.gelu(ys_ref[...].astype(F32), approximate=True)
    gate = jnp.dot(ys.astype(BF16), wglu_ref[...], preferred_element_type=F32)
    ys = ys * jax.nn.sigmoid(gate + bglu_ref[...])
    ma = jnp.dot(ya_ref[...], wa_ref[...], preferred_element_type=F32)
    ms = jnp.dot(ys.astype(BF16), ws_ref[...], preferred_element_type=F32)
    merged = (jax.nn.sigmoid(ga_ref[...].astype(F32)) * ma
              + jax.nn.sigmoid(gs_ref[...].astype(F32)) * ms)
    mix = jnp.dot(merged.astype(BF16), wo_ref[...], preferred_element_type=F32)
    h = x_ref[...] + _rms(mix, npost_ref[...])
    h_ref[...] = h
    z_ref[...] = _rms(h, npre_ref[...]).astype(z_ref.dtype)


def _merge(x2, y_a, y_s, proj, w_glu, b_glu, w_a, w_s, w_o, n_post, n_pre, *, ga_blk):
    t, d = x2.shape
    wa = y_a.shape[1]
    wsm = y_s.shape[1]
    tm = _tile(t, 256)
    row = lambda c: pl.BlockSpec((tm, c), lambda i: (i, 0))
    full = lambda a: pl.BlockSpec(a.shape, lambda i: (0, 0), pipeline_mode=pl.Buffered(1))
    return pl.pallas_call(
        _merge_kernel,
        grid=(t // tm,),
        in_specs=[row(d), row(wa), row(wsm),
                  pl.BlockSpec((tm, d), lambda i: (i, ga_blk)),
                  pl.BlockSpec((tm, d), lambda i: (i, ga_blk + 1)),
                  full(w_glu), full(b_glu), full(w_a), full(w_s), full(w_o),
                  full(n_post), full(n_pre)],
        out_specs=[row(d), row(d)],
        out_shape=[jax.ShapeDtypeStruct((t, d), F32),
                   jax.ShapeDtypeStruct((t, d), BF16)],
        compiler_params=_params("parallel"),
        name="merge",
    )(x2, y_a, y_s, proj, proj, w_glu, b_glu, w_a, w_s, w_o, n_post, n_pre)


def _ffn_kernel(z_ref, h_ref, wg_ref, wu_ref, wd_ref, npost_ref, o_ref):
    j = pl.program_id(1)

    @pl.when(j == 0)
    def _():
        o_ref[...] = jnp.zeros(o_ref.shape, o_ref.dtype)

    z = z_ref[...]
    g = jnp.dot(z, wg_ref[...], preferred_element_type=F32)
    u = jnp.dot(z, wu_ref[...], preferred_element_type=F32)
    a = (jax.nn.silu(g) * u).astype(BF16)
    o_ref[...] += jnp.dot(a, wd_ref[...], preferred_element_type=F32)

    @pl.when(j == pl.num_programs(1) - 1)
    def _():
        o_ref[...] = h_ref[...] + _rms(o_ref[...], npost_ref[...])


def _ffn(z, h, w_g, w_u, w_d, n_post):
    t, d = z.shape
    f = w_g.shape[1]
    tm = _tile(t, 512)
    tf = _tile(f, 512)
    return pl.pallas_call(
        _ffn_kernel,
        grid=(t // tm, f // tf),
        in_specs=[pl.BlockSpec((tm, d), lambda i, j: (i, 0)),
                  pl.BlockSpec((tm, d), lambda i, j: (i, 0)),
                  pl.BlockSpec((d, tf), lambda i, j: (0, j)),
                  pl.BlockSpec((d, tf), lambda i, j: (0, j)),
                  pl.BlockSpec((tf, d), lambda i, j: (j, 0)),
                  pl.BlockSpec((1, d), lambda i, j: (0, 0))],
        out_specs=pl.BlockSpec((tm, d), lambda i, j: (i, 0)),
        out_shape=jax.ShapeDtypeStruct((t, d), F32),
        compiler_params=_params("parallel", "arbitrary"),
        name="ffn",
    )(z, h, w_g, w_u, w_d, n_post)


def _lambda_init(layer_idx):
    return 0.8 - 0.6 * math.exp(-0.3 * layer_idx)


def kernel(x, w_in, lambda_q1, lambda_k1, lambda_q2, lambda_k2, subln_w, ssm_a_re, ssm_a_im, ssm_log_dt, ssm_b_re, ssm_b_im, ssm_c_re, ssm_c_im, ssm_d, w_glu, b_glu, w_attn_branch, w_ssm_branch, w_out, norm_mix_pre, norm_mix_post, w_ffn_gate, w_ffn_up, w_ffn_down, norm_ffn_pre, norm_ffn_post):
    bsz, seq, d = x.shape
    depth = w_in.shape[0]
    dh = lambda_q1.shape[-1]
    e = 2 * dh
    attn_w = w_attn_branch.shape[1]
    ssm_w = w_ssm_branch.shape[1]
    n_heads = attn_w // e
    jb = ssm_w // LANES
    tc = SSM_CHUNK
    nc = seq // tc
    seg_rows = nc // SCAN_ROWS
    t = bsz * seq
    row = lambda v: v.reshape(1, -1).astype(F32)

    tm = _tile(seq, 512)
    qscale = dh ** -0.5 * math.log2(math.e)

    h = x.reshape(t, d)
    for l in range(depth):
        lam_init = _lambda_init(l)
        wq, wk, wv, wsi, wga, wgs = jnp.split(
            w_in[l], [attn_w, 2 * attn_w, 3 * attn_w, 3 * attn_w + ssm_w, 3 * attn_w + ssm_w + d],
            axis=1)
        w_main = jnp.concatenate([wq * qscale, wk, wga, wgs], axis=1).astype(BF16)
        proj, vt3, u = _in_proj(h, row(norm_mix_pre[l]), w_main, wv.T.astype(BF16),
                                wsi.astype(BF16), tm=tm, tn=d, e=e)
        y_a = _attention(proj.reshape(bsz, seq, -1), vt3, row(lambda_q1[l]), row(lambda_k1[l]),
                         row(lambda_q2[l]), row(lambda_k2[l]),
                         subln_w[l].reshape(e, 1).astype(F32),
                         n_heads=n_heads, dh=dh, lam_init=lam_init, tq=tm)
        y_a = y_a.reshape(t, attn_w)

        tables = _ssm_tables(ssm_a_re[l], ssm_a_im[l], ssm_log_dt[l], ssm_b_re[l],
                             ssm_b_im[l], ssm_c_re[l], ssm_c_im[l], ssm_d[l],
                             seg_rows=seg_rows)
        y_s = _ssm(u, tables, bsz=bsz, seg_rows=seg_rows)
        y_s = y_s.reshape(t, ssm_w)

        h1, z = _merge(h, y_a, y_s, proj, w_glu[l].astype(BF16), row(b_glu[l]),
                       w_attn_branch[l].astype(BF16), w_ssm_branch[l].astype(BF16),
                       w_out[l].astype(BF16), row(norm_mix_post[l]), row(norm_ffn_pre[l]),
                       ga_blk=2 * attn_w // d)
        h = _ffn(z, h1, w_ffn_gate[l].astype(BF16), w_ffn_up[l].astype(BF16),
                 w_ffn_down[l].astype(BF16), row(norm_ffn_post[l]))
    return h.reshape(bsz, seq, d)
```

```python
import functools
import math

import jax
import jax.numpy as jnp
from jax import lax
from jax.experimental import pallas as pl
from jax.experimental.pallas import tpu as pltpu

EPS = 1e-6
NEG_INF = -1e30
LANES = 128
SSM_CHUNK = 8
SCAN_ROWS = 8
ONES_ROWS = 16
VMEM_LIMIT_BYTES = 56 * 1024 * 1024
F32 = jnp.float32
BF16 = jnp.bfloat16
_NT = (((1,), (1,)), ((), ()))


def _params(*sem):
    return pltpu.CompilerParams(dimension_semantics=sem,
                                vmem_limit_bytes=VMEM_LIMIT_BYTES)


def _rms(x, gain):
    ms = jnp.mean(x * x, axis=-1, keepdims=True)
    return x * lax.rsqrt(ms + EPS) * gain


def _tile(n, pref):
    t = min(n, pref)
    assert n % t == 0, (n, t)
    return t


def _in_proj_kernel(x_ref, g_ref, w_ref, wvt_ref, ws_ref, o_ref, vt_ref, uc_ref,
                    u_ref, s_ref, *, e):
    @pl.when(pl.program_id(1) == 0)
    def _():
        u = _rms(x_ref[...], g_ref[...]).astype(u_ref.dtype)
        u_ref[...] = u
        vt = lax.dot_general(wvt_ref[...], u, _NT, preferred_element_type=F32)
        ea = e + ONES_ROWS
        for hd in range(vt.shape[0] // e):
            vt_ref[hd * ea:hd * ea + e, :] = vt[hd * e:(hd + 1) * e, :].astype(vt_ref.dtype)
            vt_ref[hd * ea + e:(hd + 1) * ea, :] = jnp.ones((ONES_ROWS, vt.shape[1]), vt_ref.dtype)
        s = jnp.dot(u, ws_ref[...], preferred_element_type=F32)
        nrow = uc_ref.shape[1]
        for j in range(s_ref.shape[0]):
            s_ref[j] = s[:, j * LANES:(j + 1) * LANES]
        for j in range(s_ref.shape[0]):
            for k in range(SSM_CHUNK):
                uc_ref[j, :, k * LANES:(k + 1) * LANES] = (
                    s_ref.at[j][pl.ds(k, nrow, stride=SSM_CHUNK), :].astype(uc_ref.dtype))

    o_ref[...] = jnp.dot(u_ref[...], w_ref[...],
                         preferred_element_type=F32).astype(o_ref.dtype)


def _in_proj(x2, gain, w, wvt, w_s, *, tm, tn, e):
    t, d = x2.shape
    n = w.shape[1]
    nva = wvt.shape[0] // e * (e + ONES_ROWS)
    jb = w_s.shape[1] // LANES
    kern = functools.partial(_in_proj_kernel, e=e)
    const = lambda a: pl.BlockSpec(a.shape, lambda i, j: (0, 0), pipeline_mode=pl.Buffered(1))
    return pl.pallas_call(
        kern,
        grid=(t // tm, n // tn),
        in_specs=[pl.BlockSpec((tm, d), lambda i, j: (i, 0)),
                  pl.BlockSpec((1, d), lambda i, j: (0, 0)),
                  pl.BlockSpec((d, tn), lambda i, j: (0, j)),
                  const(wvt), const(w_s)],
        out_specs=[pl.BlockSpec((tm, tn), lambda i, j: (i, j)),
                   pl.BlockSpec((None, nva, tm), lambda i, j: (i, 0, 0)),
                   pl.BlockSpec((jb, tm // SSM_CHUNK, SSM_CHUNK * LANES), lambda i, j: (0, i, 0))],
        out_shape=[jax.ShapeDtypeStruct((t, n), BF16),
                   jax.ShapeDtypeStruct((t // tm, nva, tm), BF16),
                   jax.ShapeDtypeStruct((jb, t // SSM_CHUNK, SSM_CHUNK * LANES), BF16)],
        scratch_shapes=[pltpu.VMEM((tm, d), BF16),
                        pltpu.VMEM((jb, tm, LANES), F32)],
        compiler_params=_params("parallel", "arbitrary"),
        name="in_proj",
    )(x2, gain, w, wvt, w_s)


def _attn_kernel(q_ref, qn_ref, k_ref, vt_ref, lq1_ref, lk1_ref, lq2_ref, lk2_ref, sw_ref,
                 o_ref, qs_ref, qns_ref, m_ref, acc_ref, sa_ref, sb_ref, sc_ref,
                 *, tq, tk, dh, lam_init):
    qi = pl.program_id(2)

    def stack(src_ref, dst_ref):
        q = src_ref[...]
        lane = lax.broadcasted_iota(jnp.int32, q.shape, 1)
        zero = jnp.zeros_like(q)
        dst_ref[0:tq, :] = jnp.where(lane < dh, q, zero)
        dst_ref[tq:2 * tq, :] = jnp.where(lane >= dh, q, zero)

    stack(q_ref, qs_ref)
    stack(qn_ref, qns_ref)
    m_ref[...] = jnp.full(m_ref.shape, NEG_INF, F32)
    acc_ref[...] = jnp.zeros(acc_ref.shape, F32)

    def scores(ki, s_ref, stacked_q_ref=qs_ref):
        k = k_ref[pl.ds(pl.multiple_of(ki * tk, tk), tk), :]
        s_ref[...] = lax.dot_general(k, stacked_q_ref[...], _NT,
                                     preferred_element_type=F32)

    def scores_next():
        scores(0, sc_ref, qns_ref)

    def consume(ki, s_ref, masked):
        s = s_ref[...]
        if masked:
            kpos = ki * tk + lax.broadcasted_iota(jnp.int32, s.shape, 0)
            col = lax.broadcasted_iota(jnp.int32, s.shape, 1)
            qpos = qi * tq + jnp.where(col >= tq, col - tq, col)
            s = jnp.where(kpos <= qpos, s, NEG_INF)
        m_prev = m_ref[...]
        m_new = jnp.maximum(m_prev, jnp.max(s, axis=0, keepdims=True))
        alpha = jnp.exp2(m_prev - m_new)
        p = jnp.exp2(s - m_new)
        acc_ref[...] = alpha * acc_ref[...] + jnp.dot(
            vt_ref[ki], p.astype(vt_ref.dtype), preferred_element_type=F32)
        m_ref[...] = m_new

    @pl.when(qi == 0)
    def _():
        scores(0, sa_ref)
        scores_next()
        consume(0, sa_ref, True)

    @pl.when(qi > 0)
    def _():
        scores(1, sb_ref)
        consume(0, sc_ref, False)

        def pair(j, carry):
            scores(2 * j + 2, sa_ref)
            consume(2 * j + 1, sb_ref, False)
            scores(2 * j + 3, sb_ref)
            consume(2 * j + 2, sa_ref, False)
            return carry

        lax.fori_loop(0, (qi - 1) // 2, pair, 0)

        @pl.when(qi % 2 == 1)
        def _():
            scores_next()
            consume(qi, sb_ref, True)

        @pl.when(qi % 2 == 0)
        def _():
            scores(qi, sa_ref)
            consume(qi - 1, sb_ref, False)
            scores_next()
            consume(qi, sa_ref, True)

    lam = (jnp.exp(jnp.sum(lq1_ref[...] * lk1_ref[...], axis=-1, keepdims=True))
           - jnp.exp(jnp.sum(lq2_ref[...] * lk2_ref[...], axis=-1, keepdims=True))
           + lam_init)
    e = 2 * dh
    acc = acc_ref[...]
    o_all = acc[0:e, :] / acc[e:e + 1, :]
    o = o_all[:, 0:tq] - lam * o_all[:, tq:2 * tq]
    ms = jnp.mean(o * o, axis=0, keepdims=True)
    o = o * lax.rsqrt(ms + EPS) * sw_ref[...] * (1.0 - lam_init)
    o_ref[...] = o.T.astype(o_ref.dtype)


def _attention(proj3, vt3, lq1, lk1, lq2, lk2, subln, *, n_heads, dh, lam_init, tq):
    bsz, seq, _ = proj3.shape
    e = 2 * dh
    nk = seq // tq
    ea = e + ONES_ROWS
    assert vt3.shape == (bsz * nk, n_heads * ea, tq)
    kern = functools.partial(_attn_kernel, tq=tq, tk=tq, dh=dh, lam_init=lam_init)
    vec = pl.BlockSpec((1, dh), lambda b, h, i: (0, 0))
    return pl.pallas_call(
        kern,
        grid=(bsz, n_heads, nk),
        in_specs=[pl.BlockSpec((None, tq, e), lambda b, h, i: (b, i, h)),
                  pl.BlockSpec((None, tq, e), lambda b, h, i: (b, jnp.minimum(i + 1, nk - 1), h)),
                  pl.BlockSpec((None, seq, e), lambda b, h, i: (b, 0, n_heads + h)),
                  pl.BlockSpec((nk, ea, tq), lambda b, h, i: (b, h, 0)),
                  vec, vec, vec, vec,
                  pl.BlockSpec((e, 1), lambda b, h, i: (0, 0))],
        out_specs=pl.BlockSpec((None, tq, e), lambda b, h, i: (b, i, h)),
        out_shape=jax.ShapeDtypeStruct((bsz, seq, n_heads * e), BF16),
        scratch_shapes=[pltpu.VMEM((2 * tq, e), BF16),
                        pltpu.VMEM((2 * tq, e), BF16),
                        pltpu.VMEM((1, 2 * tq), F32),
                        pltpu.VMEM((ea, 2 * tq), F32),
                        pltpu.VMEM((tq, 2 * tq), F32),
                        pltpu.VMEM((tq, 2 * tq), F32),
                        pltpu.VMEM((tq, 2 * tq), F32)],
        compiler_params=_params("parallel", "parallel", "arbitrary"),
        name="diff_attention",
    )(proj3, proj3, proj3, vt3, lq1, lk1, lq2, lk2, subln)


def _ssm_tables(a_re, a_im, log_dt, b_re, b_im, c_re, c_im, d_skip, *, seg_rows):
    g, p, h = b_re.shape
    tc = SSM_CHUNK
    gpb = LANES // h
    jb = g // gpb
    dt = jnp.exp(log_dt)[:, None]
    er, ei = a_re * dt, a_im * dt

    def powers(n):
        n = jnp.asarray(n, F32).reshape((-1, 1, 1))
        mag = jnp.exp(n * er)
        return mag * jnp.cos(n * ei), mag * jnp.sin(n * ei)

    ar, ai = powers([1.0])
    ar, ai = ar[0], ai[0]
    den = a_re * a_re + a_im * a_im
    fr = ((ar - 1.0) * a_re + ai * a_im) / den
    fi = (ai * a_re - (ar - 1.0) * a_im) / den
    bbr = fr[..., None] * b_re - fi[..., None] * b_im
    bbi = fr[..., None] * b_im + fi[..., None] * b_re

    pr, pi = powers(jnp.arange(tc + 1))
    cpr = c_re[None] * pr[:, :, None, :] - c_im[None] * pi[:, :, None, :]
    cpi = c_re[None] * pi[:, :, None, :] + c_im[None] * pr[:, :, None, :]
    kern = (jnp.einsum('tgop,gpi->tgio', cpr[:tc], bbr)
            - jnp.einsum('tgop,gpi->tgio', cpi[:tc], bbi))
    def group_mask(rows_per_group, cols_per_group):
        r = jnp.arange(gpb * rows_per_group) // rows_per_group
        c = jnp.arange(gpb * cols_per_group) // cols_per_group
        return r[:, None] == c[None, :]

    def block_diag(x, rows_per_group, cols_per_group):
        x = jnp.tile(x, (1,) * (x.ndim - 1) + (gpb,))
        return jnp.where(group_mask(rows_per_group, cols_per_group), x, 0.0)

    kidx = jnp.arange(tc)
    tau = kidx[None, :] - kidx[:, None]
    bd = block_diag(kern.reshape(tc, jb, LANES, h), h, h)
    kt = jnp.where((tau >= 0)[:, :, None, None, None], bd[jnp.clip(tau, 0)], 0.0)
    t_mat = jnp.transpose(kt, (2, 0, 3, 1, 4)).reshape(jb, tc * LANES, tc * LANES)

    qr, qi_ = pr[tc - 1 - kidx], pi[tc - 1 - kidx]
    sr = qr[:, :, None, :] * jnp.swapaxes(bbr, 1, 2)[None] - qi_[:, :, None, :] * jnp.swapaxes(bbi, 1, 2)[None]
    si = qr[:, :, None, :] * jnp.swapaxes(bbi, 1, 2)[None] + qi_[:, :, None, :] * jnp.swapaxes(bbr, 1, 2)[None]
    st = jnp.stack([sr, si], 0).reshape(2, tc, jb, LANES, p)
    st = block_diag(st, h, p)
    w_st = jnp.transpose(st, (2, 1, 3, 0, 4)).reshape(jb, tc * LANES, 2 * gpb * p)

    ot = jnp.stack([cpr[1:], -cpi[1:]], 0)
    ot = jnp.swapaxes(ot, 3, 4).reshape(2, tc, jb, gpb * p, h)
    ot = block_diag(ot, p, h)
    w_out = jnp.transpose(ot, (2, 0, 3, 1, 4)).reshape(jb, 2 * gpb * p, tc * LANES)

    def rows(n):
        xr, xi = powers(n)
        x = jnp.stack([xr, xi], 1).reshape(-1, 2, jb, gpb * p)
        return jnp.transpose(x, (2, 0, 1, 3)).reshape(jb, -1, 2 * gpb * p)

    a_chunk = rows([float(tc)])
    a_seg = rows([float(tc * seg_rows)])
    p_pow = rows(tc * jnp.arange(seg_rows))
    d_row = jnp.tile(d_skip.reshape(jb, 1, LANES), (1, 1, tc))
    return (t_mat.astype(BF16), w_st.astype(BF16), w_out.astype(BF16),
            a_chunk, a_seg, p_pow, d_row)


def _cmul(ar, ai, xr, xi):
    return ar * xr - ai * xi, ar * xi + ai * xr


def _ssm_kernel(u_ref, t_ref, wst_ref, wout_ref, ach_ref, aseg_ref, pp_ref, d_ref,
                y_ref, xst_ref, e_ref, sp_ref, z_ref, *, seg_rows):
    ncb = xst_ref.shape[0]
    hb = ncb // 2
    pitch = xst_ref.shape[1] // SCAN_ROWS
    blk = lambda c: slice(c * LANES, (c + 1) * LANES)
    u = u_ref[...]
    xst = jnp.dot(u, wst_ref[...], preferred_element_type=F32)
    for c in range(ncb):
        for s in range(SCAN_ROWS):
            xst_ref[c, s * pitch:s * pitch + seg_rows, :] = (
                xst[s * seg_rows:(s + 1) * seg_rows, blk(c)])

    ach = ach_ref[...]
    a_ch = [jnp.broadcast_to(ach[:, blk(c)], (SCAN_ROWS, LANES)) for c in range(ncb)]

    def cstep(a, e, c):
        return (a[c] * e[c] - a[c + hb] * e[c + hb], a[c] * e[c + hb] + a[c + hb] * e[c])

    def scan_body(i, e):
        row0 = pl.multiple_of(i * SCAN_ROWS, SCAN_ROWS)
        new = [None] * ncb
        for c in range(hb):
            e_ref[c, pl.ds(row0, SCAN_ROWS), :] = e[c]
            e_ref[c + hb, pl.ds(row0, SCAN_ROWS), :] = e[c + hb]
            nr, ni = cstep(a_ch, e, c)
            new[c] = nr + xst_ref.at[c][pl.ds(i, SCAN_ROWS, stride=pitch), :]
            new[c + hb] = ni + xst_ref.at[c + hb][pl.ds(i, SCAN_ROWS, stride=pitch), :]
        return tuple(new)

    zero = jnp.zeros((SCAN_ROWS, LANES), F32)
    end = lax.fori_loop(0, seg_rows, scan_body, (zero,) * ncb, unroll=4)

    asg = aseg_ref[...]
    a_sg = [asg[:, blk(c)] for c in range(ncb)]
    car = [jnp.zeros((1, LANES), F32)] * ncb
    pp = pp_ref[...]
    p_pw = [pp[:, blk(c)] for c in range(ncb)]
    for s in range(SCAN_ROWS):
        rows = slice(s * seg_rows, (s + 1) * seg_rows)
        nxt = [None] * ncb
        for c in range(hb):
            fr, fi = cstep(p_pw, car, c)
            sp_ref[rows, blk(c)] = (
                e_ref.at[c][pl.ds(s, seg_rows, stride=SCAN_ROWS), :] + fr).astype(sp_ref.dtype)
            sp_ref[rows, blk(c + hb)] = (
                e_ref.at[c + hb][pl.ds(s, seg_rows, stride=SCAN_ROWS), :] + fi).astype(sp_ref.dtype)
            nr, ni = cstep(a_sg, car, c)
            nxt[c] = nr + end[c][s:s + 1, :]
            nxt[c + hb] = ni + end[c + hb][s:s + 1, :]
        car = nxt

    y = jnp.dot(u, t_ref[...], preferred_element_type=F32)
    y = y + jnp.dot(sp_ref[...], wout_ref[...], preferred_element_type=F32)
    y = y + d_ref[...] * u.astype(F32)

    tc = SSM_CHUNK
    ng = y.shape[0] // SCAN_ROWS
    for k in range(tc):
        z_ref[:, k * SCAN_ROWS:(k + 1) * SCAN_ROWS, :] = (
            y[:, blk(k)].reshape(ng, SCAN_ROWS, LANES))
    for c in range(SCAN_ROWS):
        y_ref[:, c * tc:(c + 1) * tc, :] = z_ref[:, pl.ds(c, tc, stride=SCAN_ROWS), :]


def _ssm(u, tables, *, bsz, seg_rows):
    t_mat, w_st, w_out, a_chunk, a_seg, p_pow, d_row = tables
    jb, m, kc = u.shape
    nc = m // bsz
    s2 = w_st.shape[2]
    assert nc == SCAN_ROWS * seg_rows
    kern = functools.partial(_ssm_kernel, seg_rows=seg_rows)
    wspec = lambda r, c: pl.BlockSpec((None, r, c), lambda j, b: (j, 0, 0))
    return pl.pallas_call(
        kern,
        grid=(jb, bsz),
        in_specs=[pl.BlockSpec((None, nc, kc), lambda j, b: (j, b, 0)),
                  wspec(kc, kc), wspec(kc, s2), wspec(s2, kc),
                  wspec(1, s2), wspec(1, s2), wspec(seg_rows, s2), wspec(1, kc)],
        out_specs=pl.BlockSpec((nc // SCAN_ROWS, SCAN_ROWS * SSM_CHUNK, LANES),
                               lambda j, b: (b, 0, j)),
        out_shape=jax.ShapeDtypeStruct((m // SCAN_ROWS, SCAN_ROWS * SSM_CHUNK, jb * LANES), F32),
        scratch_shapes=[pltpu.VMEM((s2 // LANES, SCAN_ROWS * (seg_rows + 8), LANES), F32),
                        pltpu.VMEM((s2 // LANES, nc, LANES), F32),
                        pltpu.VMEM((nc, s2), BF16),
                        pltpu.VMEM((nc // SCAN_ROWS, SCAN_ROWS * SSM_CHUNK, LANES), F32)],
        compiler_params=_params("parallel", "arbitrary"),
        name="s5_scan",
    )(u, t_mat, w_st, w_out, a_chunk, a_seg, p_pow, d_row)


def _merge_kernel(x_ref, ya_ref, ys_ref, ga_ref, gs_ref, wglu_ref, bglu_ref, wa_ref,
                  ws_ref, wo_ref, npost_ref, npre_ref, h_ref, z_ref):
    ys = jax.nn.gelu(ys_ref[...].astype(F32), approximate=True)
    gate = jnp.dot(ys.astype(BF16), wglu_ref[...], preferred_element_type=F32)
    ys = ys * jax.nn.sigmoid(gate + bglu_ref[...])
    ma = jnp.dot(ya_ref[...], wa_ref[...], preferred_element_type=F32)
    ms = jnp.dot(ys.astype(BF16), ws_ref[...], preferred_element_type=F32)
    merged = (jax.nn.sigmoid(ga_ref[...].astype(F32)) * ma
              + jax.nn.sigmoid(gs_ref[...].astype(F32)) * ms)
    mix = jnp.dot(merged.astype(BF16), wo_ref[...], preferred_element_type=F32)
    h = x_ref[...] + _rms(mix, npost_ref[...])
    h_ref[...] = h
    z_ref[...] = _rms(h, npre_ref[...]).astype(z_ref.dtype)


def _merge(x2, y_a, y_s, proj, w_glu, b_glu, w_a, w_s, w_o, n_post, n_pre, *, ga_blk):
    t, d = x2.shape
    wa = y_a.shape[1]
    wsm = y_s.shape[1]
    tm = _tile(t, 256)
    row = lambda c: pl.BlockSpec((tm, c), lambda i: (i, 0))
    full = lambda a: pl.BlockSpec(a.shape, lambda i: (0, 0), pipeline_mode=pl.Buffered(1))
    return pl.pallas_call(
        _merge_kernel,
        grid=(t // tm,),
        in_specs=[row(d), row(wa), row(wsm),
                  pl.BlockSpec((tm, d), lambda i: (i, ga_blk)),
                  pl.BlockSpec((tm, d), lambda i: (i, ga_blk + 1)),
                  full(w_glu), full(b_glu), full(w_a), full(w_s), full(w_o),
                  full(n_post), full(n_pre)],
        out_specs=[row(d), row(d)],
        out_shape=[jax.ShapeDtypeStruct((t, d), F32),
                   jax.ShapeDtypeStruct((t, d), BF16)],
        compiler_params=_params("parallel"),
        name="merge",
    )(x2, y_a, y_s, proj, proj, w_glu, b_glu, w_a, w_s, w_o, n_post, n_pre)


def _ffn_kernel(z_ref, h_ref, wg_ref, wu_ref, wd_ref, npost_ref, o_ref):
    j = pl.program_id(1)

    @pl.when(j == 0)
    def _():
        o_ref[...] = jnp.zeros(o_ref.shape, o_ref.dtype)

    z = z_ref[...]
    g = jnp.dot(z, wg_ref[...], preferred_element_type=F32)
    u = jnp.dot(z, wu_ref[...], preferred_element_type=F32)
    a = (jax.nn.silu(g) * u).astype(BF16)
    o_ref[...] += jnp.dot(a, wd_ref[...], preferred_element_type=F32)

    @pl.when(j == pl.num_programs(1) - 1)
    def _():
        o_ref[...] = h_ref[...] + _rms(o_ref[...], npost_ref[...])


def _ffn(z, h, w_g, w_u, w_d, n_post):
    t, d = z.shape
    f = w_g.shape[1]
    tm = _tile(t, 512)
    tf = _tile(f, 512)
    return pl.pallas_call(
        _ffn_kernel,
        grid=(t // tm, f // tf),
        in_specs=[pl.BlockSpec((tm, d), lambda i, j: (i, 0)),
                  pl.BlockSpec((tm, d), lambda i, j: (i, 0)),
                  pl.BlockSpec((d, tf), lambda i, j: (0, j)),
                  pl.BlockSpec((d, tf), lambda i, j: (0, j)),
                  pl.BlockSpec((tf, d), lambda i, j: (j, 0)),
                  pl.BlockSpec((1, d), lambda i, j: (0, 0))],
        out_specs=pl.BlockSpec((tm, d), lambda i, j: (i, 0)),
        out_shape=jax.ShapeDtypeStruct((t, d), F32),
        compiler_params=_params("parallel", "arbitrary"),
        name="ffn",
    )(z, h, w_g, w_u, w_d, n_post)


def _lambda_init(layer_idx):
    return 0.8 - 0.6 * math.exp(-0.3 * layer_idx)


def kernel(x, w_in, lambda_q1, lambda_k1, lambda_q2, lambda_k2, subln_w, ssm_a_re, ssm_a_im, ssm_log_dt, ssm_b_re, ssm_b_im, ssm_c_re, ssm_c_im, ssm_d, w_glu, b_glu, w_attn_branch, w_ssm_branch, w_out, norm_mix_pre, norm_mix_post, w_ffn_gate, w_ffn_up, w_ffn_down, norm_ffn_pre, norm_ffn_post):
    bsz, seq, d = x.shape
    depth = w_in.shape[0]
    dh = lambda_q1.shape[-1]
    e = 2 * dh
    attn_w = w_attn_branch.shape[1]
    ssm_w = w_ssm_branch.shape[1]
    n_heads = attn_w // e
    jb = ssm_w // LANES
    tc = SSM_CHUNK
    nc = seq // tc
    seg_rows = nc // SCAN_ROWS
    t = bsz * seq
    row = lambda v: v.reshape(1, -1).astype(F32)

    tm = _tile(seq, 512)
    qscale = dh ** -0.5 * math.log2(math.e)

    h = x.reshape(t, d)
    for l in range(depth):
        lam_init = _lambda_init(l)
        wq, wk, wv, wsi, wga, wgs = jnp.split(
            w_in[l], [attn_w, 2 * attn_w, 3 * attn_w, 3 * attn_w + ssm_w, 3 * attn_w + ssm_w + d],
            axis=1)
        w_main = jnp.concatenate([wq * qscale, wk, wga, wgs], axis=1).astype(BF16)
        proj, vt3, u = _in_proj(h, row(norm_mix_pre[l]), w_main, wv.T.astype(BF16),
                                wsi.astype(BF16), tm=tm, tn=d, e=e)
        y_a = _attention(proj.reshape(bsz, seq, -1), vt3, row(lambda_q1[l]), row(lambda_k1[l]),
                         row(lambda_q2[l]), row(lambda_k2[l]),
                         subln_w[l].reshape(e, 1).astype(F32),
                         n_heads=n_heads, dh=dh, lam_init=lam_init, tq=tm)
        y_a = y_a.reshape(t, attn_w)

        tables = _ssm_tables(ssm_a_re[l], ssm_a_im[l], ssm_log_dt[l], ssm_b_re[l],
                             ssm_b_im[l], ssm_c_re[l], ssm_c_im[l], ssm_d[l],
                             seg_rows=seg_rows)
        y_s = _ssm(u, tables, bsz=bsz, seg_rows=seg_rows)
        y_s = y_s.reshape(t, ssm_w)

        h1, z = _merge(h, y_a, y_s, proj, w_glu[l].astype(BF16), row(b_glu[l]),
                       w_attn_branch[l].astype(BF16), w_ssm_branch[l].astype(BF16),
                       w_out[l].astype(BF16), row(norm_mix_post[l]), row(norm_ffn_pre[l]),
                       ga_blk=2 * attn_w // d)
        h = _ffn(z, h1, w_ffn_gate[l].astype(BF16), w_ffn_up[l].astype(BF16),
                 w_ffn_down[l].astype(BF16), row(norm_ffn_post[l]))
    return h.reshape(bsz, seq, d)
```

```python
import functools
import math

import jax
import jax.numpy as jnp
from jax import lax
from jax.experimental import pallas as pl
from jax.experimental.pallas import tpu as pltpu

EPS = 1e-6
NEG_INF = -1e30
LANES = 128
SSM_CHUNK = 8
SCAN_ROWS = 8
ONES_ROWS = 16
VMEM_LIMIT_BYTES = 58 * 1024 * 1024
F32 = jnp.float32
BF16 = jnp.bfloat16
_NT = (((1,), (1,)), ((), ()))


def _params(*sem):
    return pltpu.CompilerParams(dimension_semantics=sem,
                                vmem_limit_bytes=VMEM_LIMIT_BYTES)


def _rms(x, gain):
    ms = jnp.mean(x * x, axis=-1, keepdims=True)
    return x * lax.rsqrt(ms + EPS) * gain


def _tile(n, pref):
    t = min(n, pref)
    assert n % t == 0, (n, t)
    return t


def _in_proj_kernel(x_ref, g_ref, w_ref, wvt_ref, ws_ref, o_ref, vt_ref, uc_ref,
                    u_ref, s_ref, *, e):
    @pl.when(pl.program_id(1) == 0)
    def _():
        u = _rms(x_ref[...], g_ref[...]).astype(u_ref.dtype)
        u_ref[...] = u
        vt = lax.dot_general(wvt_ref[...], u, _NT, preferred_element_type=F32)
        ea = e + ONES_ROWS
        for hd in range(vt.shape[0] // e):
            vt_ref[hd * ea:hd * ea + e, :] = vt[hd * e:(hd + 1) * e, :].astype(vt_ref.dtype)
            vt_ref[hd * ea + e:(hd + 1) * ea, :] = jnp.ones((ONES_ROWS, vt.shape[1]), vt_ref.dtype)
        s = jnp.dot(u, ws_ref[...], preferred_element_type=F32)
        nrow = uc_ref.shape[1]
        for j in range(s_ref.shape[0]):
            s_ref[j] = s[:, j * LANES:(j + 1) * LANES]
        for j in range(s_ref.shape[0]):
            for k in range(SSM_CHUNK):
                uc_ref[j, :, k * LANES:(k + 1) * LANES] = (
                    s_ref.at[j][pl.ds(k, nrow, stride=SSM_CHUNK), :].astype(uc_ref.dtype))

    o_ref[...] = jnp.dot(u_ref[...], w_ref[...],
                         preferred_element_type=F32).astype(o_ref.dtype)


def _in_proj(x2, gain, w, wvt, w_s, *, tm, tn, e):
    t, d = x2.shape
    n = w.shape[1]
    nva = wvt.shape[0] // e * (e + ONES_ROWS)
    jb = w_s.shape[1] // LANES
    kern = functools.partial(_in_proj_kernel, e=e)
    const = lambda a: pl.BlockSpec(a.shape, lambda i, j: (0, 0), pipeline_mode=pl.Buffered(1))
    return pl.pallas_call(
        kern,
        grid=(t // tm, n // tn),
        in_specs=[pl.BlockSpec((tm, d), lambda i, j: (i, 0)),
                  pl.BlockSpec((1, d), lambda i, j: (0, 0)),
                  pl.BlockSpec((d, tn), lambda i, j: (0, j)),
                  const(wvt), const(w_s)],
        out_specs=[pl.BlockSpec((tm, tn), lambda i, j: (i, j)),
                   pl.BlockSpec((None, nva, tm), lambda i, j: (i, 0, 0)),
                   pl.BlockSpec((jb, tm // SSM_CHUNK, SSM_CHUNK * LANES), lambda i, j: (0, i, 0))],
        out_shape=[jax.ShapeDtypeStruct((t, n), BF16),
                   jax.ShapeDtypeStruct((t // tm, nva, tm), BF16),
                   jax.ShapeDtypeStruct((jb, t // SSM_CHUNK, SSM_CHUNK * LANES), BF16)],
        scratch_shapes=[pltpu.VMEM((tm, d), BF16),
                        pltpu.VMEM((jb, tm, LANES), F32)],
        compiler_params=_params("parallel", "arbitrary"),
        name="in_proj",
    )(x2, gain, w, wvt, w_s)


def _attn_kernel(q_ref, qn_ref, k_ref, vt_ref, lq1_ref, lk1_ref, lq2_ref, lk2_ref, sw_ref,
                 o_ref, qs_ref, qns_ref, m_ref, acc_ref, sa_ref, sb_ref, sc_ref,
                 *, tq, tk, dh, lam_init):
    qi = pl.program_id(2)

    def stack(src_ref, dst_ref):
        qt = src_ref[...].astype(F32).T
        row = lax.broadcasted_iota(jnp.int32, qt.shape, 0)
        zero = jnp.zeros_like(qt)
        dst_ref[:, 0:tq] = jnp.where(row < dh, qt, zero).astype(dst_ref.dtype)
        dst_ref[:, tq:2 * tq] = jnp.where(row >= dh, qt, zero).astype(dst_ref.dtype)

    stack(q_ref, qs_ref)
    stack(qn_ref, qns_ref)
    m_ref[...] = jnp.full(m_ref.shape, NEG_INF, F32)
    acc_ref[...] = jnp.zeros(acc_ref.shape, F32)

    def scores(ki, s_ref, stacked_q_ref=qs_ref):
        k = k_ref[pl.ds(pl.multiple_of(ki * tk, tk), tk), :]
        s_ref[...] = jnp.dot(k, stacked_q_ref[...],
                             preferred_element_type=F32)

    def scores_next():
        scores(0, sc_ref, qns_ref)

    def consume(ki, s_ref, masked):
        s = s_ref[...]
        if masked:
            kpos = ki * tk + lax.broadcasted_iota(jnp.int32, s.shape, 0)
            col = lax.broadcasted_iota(jnp.int32, s.shape, 1)
            qpos = qi * tq + jnp.where(col >= tq, col - tq, col)
            s = jnp.where(kpos <= qpos, s, NEG_INF)
        m_prev = m_ref[...]
        m_new = jnp.maximum(m_prev, jnp.max(s, axis=0, keepdims=True))
        alpha = jnp.exp2(m_prev - m_new)
        p = jnp.exp2(s - m_new)
        acc_ref[...] = alpha * acc_ref[...] + jnp.dot(
            vt_ref[ki], p.astype(vt_ref.dtype), preferred_element_type=F32)
        m_ref[...] = m_new

    @pl.when(qi == 0)
    def _():
        scores(0, sa_ref)
        scores_next()
        consume(0, sa_ref, True)

    @pl.when(qi > 0)
    def _():
        scores(1, sb_ref)
        consume(0, sc_ref, False)

        def pair(j, carry):
            scores(2 * j + 2, sa_ref)
            consume(2 * j + 1, sb_ref, False)
            scores(2 * j + 3, sb_ref)
            consume(2 * j + 2, sa_ref, False)
            return carry

        lax.fori_loop(0, (qi - 1) // 2, pair, 0)

        @pl.when(qi % 2 == 1)
        def _():
            scores_next()
            consume(qi, sb_ref, True)

        @pl.when(qi % 2 == 0)
        def _():
            scores(qi, sa_ref)
            consume(qi - 1, sb_ref, False)
            scores_next()
            consume(qi, sa_ref, True)

    lam = (jnp.exp(jnp.sum(lq1_ref[...] * lk1_ref[...], axis=-1, keepdims=True))
           - jnp.exp(jnp.sum(lq2_ref[...] * lk2_ref[...], axis=-1, keepdims=True))
           + lam_init)
    e = 2 * dh
    acc = acc_ref[...]
    o_all = acc[0:e, :] / acc[e:e + 1, :]
    o = o_all[:, 0:tq] - lam * o_all[:, tq:2 * tq]
    ms = jnp.mean(o * o, axis=0, keepdims=True)
    o = o * lax.rsqrt(ms + EPS) * sw_ref[...] * (1.0 - lam_init)
    o_ref[...] = o.T.astype(o_ref.dtype)


def _attention(proj3, vt3, lq1, lk1, lq2, lk2, subln, *, n_heads, dh, lam_init, tq):
    bsz, seq, _ = proj3.shape
    e = 2 * dh
    nk = seq // tq
    ea = e + ONES_ROWS
    assert vt3.shape == (bsz * nk, n_heads * ea, tq)
    kern = functools.partial(_attn_kernel, tq=tq, tk=tq, dh=dh, lam_init=lam_init)
    vec = pl.BlockSpec((1, dh), lambda b, h, i: (0, 0))
    return pl.pallas_call(
        kern,
        grid=(bsz, n_heads, nk),
        in_specs=[pl.BlockSpec((None, tq, e), lambda b, h, i: (b, i, h)),
                  pl.BlockSpec((None, tq, e), lambda b, h, i: (b, jnp.minimum(i + 1, nk - 1), h)),
                  pl.BlockSpec((None, seq, e), lambda b, h, i: (b, 0, n_heads + h)),
                  pl.BlockSpec((nk, ea, tq), lambda b, h, i: (b, h, 0)),
                  vec, vec, vec, vec,
                  pl.BlockSpec((e, 1), lambda b, h, i: (0, 0))],
        out_specs=pl.BlockSpec((None, tq, e), lambda b, h, i: (b, i, h)),
        out_shape=jax.ShapeDtypeStruct((bsz, seq, n_heads * e), BF16),
        scratch_shapes=[pltpu.VMEM((e, 2 * tq), BF16),
                        pltpu.VMEM((e, 2 * tq), BF16),
                        pltpu.VMEM((1, 2 * tq), F32),
                        pltpu.VMEM((ea, 2 * tq), F32),
                        pltpu.VMEM((tq, 2 * tq), F32),
                        pltpu.VMEM((tq, 2 * tq), F32),
                        pltpu.VMEM((tq, 2 * tq), F32)],
        compiler_params=_params("parallel", "parallel", "arbitrary"),
        name="diff_attention",
    )(proj3, proj3, proj3, vt3, lq1, lk1, lq2, lk2, subln)


def _ssm_tables(a_re, a_im, log_dt, b_re, b_im, c_re, c_im, d_skip, *, seg_rows):
    g, p, h = b_re.shape
    tc = SSM_CHUNK
    gpb = LANES // h
    jb = g // gpb
    dt = jnp.exp(log_dt)[:, None]
    er, ei = a_re * dt, a_im * dt

    def powers(n):
        n = jnp.asarray(n, F32).reshape((-1, 1, 1))
        mag = jnp.exp(n * er)
        return mag * jnp.cos(n * ei), mag * jnp.sin(n * ei)

    ar, ai = powers([1.0])
    ar, ai = ar[0], ai[0]
    den = a_re * a_re + a_im * a_im
    fr = ((ar - 1.0) * a_re + ai * a_im) / den
    fi = (ai * a_re - (ar - 1.0) * a_im) / den
    bbr = fr[..., None] * b_re - fi[..., None] * b_im
    bbi = fr[..., None] * b_im + fi[..., None] * b_re

    pr, pi = powers(jnp.arange(tc + 1))
    cpr = c_re[None] * pr[:, :, None, :] - c_im[None] * pi[:, :, None, :]
    cpi = c_re[None] * pi[:, :, None, :] + c_im[None] * pr[:, :, None, :]
    kern = (jnp.einsum('tgop,gpi->tgio', cpr[:tc], bbr)
            - jnp.einsum('tgop,gpi->tgio', cpi[:tc], bbi))
    def group_mask(rows_per_group, cols_per_group):
        r = jnp.arange(gpb * rows_per_group) // rows_per_group
        c = jnp.arange(gpb * cols_per_group) // cols_per_group
        return r[:, None] == c[None, :]

    def block_diag(x, rows_per_group, cols_per_group):
        x = jnp.tile(x, (1,) * (x.ndim - 1) + (gpb,))
        return jnp.where(group_mask(rows_per_group, cols_per_group), x, 0.0)

    kidx = jnp.arange(tc)
    tau = kidx[None, :] - kidx[:, None]
    bd = block_diag(kern.reshape(tc, jb, LANES, h), h, h)
    kt = jnp.where((tau >= 0)[:, :, None, None, None], bd[jnp.clip(tau, 0)], 0.0)
    t_mat = jnp.transpose(kt, (2, 0, 3, 1, 4)).reshape(jb, tc * LANES, tc * LANES)

    qr, qi_ = pr[tc - 1 - kidx], pi[tc - 1 - kidx]
    sr = qr[:, :, None, :] * jnp.swapaxes(bbr, 1, 2)[None] - qi_[:, :, None, :] * jnp.swapaxes(bbi, 1, 2)[None]
    si = qr[:, :, None, :] * jnp.swapaxes(bbi, 1, 2)[None] + qi_[:, :, None, :] * jnp.swapaxes(bbr, 1, 2)[None]
    st = jnp.stack([sr, si], 0).reshape(2, tc, jb, LANES, p)
    st = block_diag(st, h, p)
    w_st = jnp.transpose(st, (2, 1, 3, 0, 4)).reshape(jb, tc * LANES, 2 * gpb * p)

    ot = jnp.stack([cpr[1:], -cpi[1:]], 0)
    ot = jnp.swapaxes(ot, 3, 4).reshape(2, tc, jb, gpb * p, h)
    ot = block_diag(ot, p, h)
    w_out = jnp.transpose(ot, (2, 0, 3, 1, 4)).reshape(jb, 2 * gpb * p, tc * LANES)

    def rows(n):
        xr, xi = powers(n)
        x = jnp.stack([xr, xi], 1).reshape(-1, 2, jb, gpb * p)
        return jnp.transpose(x, (2, 0, 1, 3)).reshape(jb, -1, 2 * gpb * p)

    a_chunk = rows([float(tc)])
    a_seg = rows([float(tc * seg_rows)])
    p_pow = rows(tc * jnp.arange(seg_rows))
    d_row = jnp.tile(d_skip.reshape(jb, 1, LANES), (1, 1, tc))
    return (t_mat.astype(BF16), w_st.astype(BF16), w_out.astype(BF16),
            a_chunk, a_seg, p_pow, d_row)


def _cmul(ar, ai, xr, xi):
    return ar * xr - ai * xi, ar * xi + ai * xr


def _ssm_kernel(u_ref, t_ref, wst_ref, wout_ref, ach_ref, aseg_ref, pp_ref, d_ref,
                y_ref, xst_ref, e_ref, sp_ref, z_ref, *, seg_rows):
    ncb = xst_ref.shape[0]
    hb = ncb // 2
    pitch = xst_ref.shape[1] // SCAN_ROWS
    blk = lambda c: slice(c * LANES, (c + 1) * LANES)
    u = u_ref[...]
    xst = jnp.dot(u, wst_ref[...], preferred_element_type=F32)
    for c in range(ncb):
        for s in range(SCAN_ROWS):
            xst_ref[c, s * pitch:s * pitch + seg_rows, :] = (
                xst[s * seg_rows:(s + 1) * seg_rows, blk(c)])

    ach = ach_ref[...]
    a_ch = [jnp.broadcast_to(ach[:, blk(c)], (SCAN_ROWS, LANES)) for c in range(ncb)]

    def cstep(a, e, c):
        return (a[c] * e[c] - a[c + hb] * e[c + hb], a[c] * e[c + hb] + a[c + hb] * e[c])

    def scan_body(i, e):
        row0 = pl.multiple_of(i * SCAN_ROWS, SCAN_ROWS)
        new = [None] * ncb
        for c in range(hb):
            e_ref[c, pl.ds(row0, SCAN_ROWS), :] = e[c]
            e_ref[c + hb, pl.ds(row0, SCAN_ROWS), :] = e[c + hb]
            nr, ni = cstep(a_ch, e, c)
            new[c] = nr + xst_ref.at[c][pl.ds(i, SCAN_ROWS, stride=pitch), :]
            new[c + hb] = ni + xst_ref.at[c + hb][pl.ds(i, SCAN_ROWS, stride=pitch), :]
        return tuple(new)

    zero = jnp.zeros((SCAN_ROWS, LANES), F32)
    end = lax.fori_loop(0, seg_rows, scan_body, (zero,) * ncb, unroll=4)

    asg = aseg_ref[...]
    a_sg = [asg[:, blk(c)] for c in range(ncb)]
    car = [jnp.zeros((1, LANES), F32)] * ncb
    pp = pp_ref[...]
    p_pw = [pp[:, blk(c)] for c in range(ncb)]
    for s in range(SCAN_ROWS):
        rows = slice(s * seg_rows, (s + 1) * seg_rows)
        nxt = [None] * ncb
        for c in range(hb):
            fr, fi = cstep(p_pw, car, c)
            sp_ref[rows, blk(c)] = (
                e_ref.at[c][pl.ds(s, seg_rows, stride=SCAN_ROWS), :] + fr).astype(sp_ref.dtype)
            sp_ref[rows, blk(c + hb)] = (
                e_ref.at[c + hb][pl.ds(s, seg_rows, stride=SCAN_ROWS), :] + fi).astype(sp_ref.dtype)
            nr, ni = cstep(a_sg, car, c)
            nxt[c] = nr + end[c][s:s + 1, :]
            nxt[c + hb] = ni + end[c + hb][s:s + 1, :]
        car = nxt

    y = jnp.dot(u, t_ref[...], preferred_element_type=F32)
    y = y + jnp.dot(sp_ref[...], wout_ref[...], preferred_element_type=F32)
    y = y + d_ref[...] * u.astype(F32)

    tc = SSM_CHUNK
    ng = y.shape[0] // SCAN_ROWS
    for k in range(tc):
        z_ref[:, k * SCAN_ROWS:(k + 1) * SCAN_ROWS, :] = (
            y[:, blk(k)].reshape(ng, SCAN_ROWS, LANES))
    for c in range(SCAN_ROWS):
        y_ref[:, c * tc:(c + 1) * tc, :] = z_ref[:, pl.ds(c, tc, stride=SCAN_ROWS), :]


def _ssm(u, tables, *, bsz, seg_rows):
    t_mat, w_st, w_out, a_chunk, a_seg, p_pow, d_row = tables
    jb, m, kc = u.shape
    nc = m // bsz
    s2 = w_st.shape[2]
    assert nc == SCAN_ROWS * seg_rows
    kern = functools.partial(_ssm_kernel, seg_rows=seg_rows)
    wspec = lambda r, c: pl.BlockSpec((None, r, c), lambda j, b: (j, 0, 0))
    return pl.pallas_call(
        kern,
        grid=(jb, bsz),
        in_specs=[pl.BlockSpec((None, nc, kc), lambda j, b: (j, b, 0)),
                  wspec(kc, kc), wspec(kc, s2), wspec(s2, kc),
                  wspec(1, s2), wspec(1, s2), wspec(seg_rows, s2), wspec(1, kc)],
        out_specs=pl.BlockSpec((nc // SCAN_ROWS, SCAN_ROWS * SSM_CHUNK, LANES),
                               lambda j, b: (b, 0, j)),
        out_shape=jax.ShapeDtypeStruct((m // SCAN_ROWS, SCAN_ROWS * SSM_CHUNK, jb * LANES), F32),
        scratch_shapes=[pltpu.VMEM((s2 // LANES, SCAN_ROWS * (seg_rows + 8), LANES), F32),
                        pltpu.VMEM((s2 // LANES, nc, LANES), F32),
                        pltpu.VMEM((nc, s2), BF16),
                        pltpu.VMEM((nc // SCAN_ROWS, SCAN_ROWS * SSM_CHUNK, LANES), F32)],
        compiler_params=_params("parallel", "arbitrary"),
        name="s5_scan",
    )(u, t_mat, w_st, w_out, a_chunk, a_seg, p_pow, d_row)


def _merge_kernel(x_ref, ya_ref, ys_ref, ga_ref, gs_ref, wglu_ref, bglu_ref, wa_ref,
                  ws_ref, wo_ref, npost_ref, npre_ref, h_ref, z_ref):
    ys = jax.nn.gelu(ys_ref[...].astype(F32), approximate=True)
    gate = jnp.dot(ys.astype(BF16), wglu_ref[...], preferred_element_type=F32)
    ys = ys * jax.nn.sigmoid(gate + bglu_ref[...])
    ma = jnp.dot(ya_ref[...], wa_ref[...], preferred_element_type=F32)
    ms = jnp.dot(ys.astype(BF16), ws_ref[...], preferred_element_type=F32)
    merged = (jax.nn.sigmoid(ga_ref[...].astype(F32)) * ma
              + jax.nn.sigmoid(gs_ref[...].astype(F32)) * ms)
    mix = jnp.dot(merged.astype(BF16), wo_ref[...], preferred_element_type=F32)
    h = x_ref[...] + _rms(mix, npost_ref[...])
    h_ref[...] = h
    z_ref[...] = _rms(h, npre_ref[...]).astype(z_ref.dtype)


def _merge(x2, y_a, y_s, proj, w_glu, b_glu, w_a, w_s, w_o, n_post, n_pre, *, ga_blk):
    t, d = x2.shape
    wa = y_a.shape[1]
    wsm = y_s.shape[1]
    tm = _tile(t, 256)
    row = lambda c: pl.BlockSpec((tm, c), lambda i: (i, 0))
    full = lambda a: pl.BlockSpec(a.shape, lambda i: (0, 0), pipeline_mode=pl.Buffered(1))
    return pl.pallas_call(
        _merge_kernel,
        grid=(t // tm,),
        in_specs=[row(d), row(wa), row(wsm),
                  pl.BlockSpec((tm, d), lambda i: (i, ga_blk)),
                  pl.BlockSpec((tm, d), lambda i: (i, ga_blk + 1)),
                  full(w_glu), full(b_glu), full(w_a), full(w_s), full(w_o),
                  full(n_post), full(n_pre)],
        out_specs=[row(d), row(d)],
        out_shape=[jax.ShapeDtypeStruct((t, d), F32),
                   jax.ShapeDtypeStruct((t, d), BF16)],
        compiler_params=_params("parallel"),
        name="merge",
    )(x2, y_a, y_s, proj, proj, w_glu, b_glu, w_a, w_s, w_o, n_post, n_pre)


def _ffn_kernel(z_ref, h_ref, wg_ref, wu_ref, wd_ref, npost_ref, o_ref):
    j = pl.program_id(1)

    @pl.when(j == 0)
    def _():
        o_ref[...] = jnp.zeros(o_ref.shape, o_ref.dtype)

    z = z_ref[...]
    g = jnp.dot(z, wg_ref[...], preferred_element_type=F32)
    u = jnp.dot(z, wu_ref[...], preferred_element_type=F32)
    a = (jax.nn.silu(g) * u).astype(BF16)
    o_ref[...] += jnp.dot(a, wd_ref[...], preferred_element_type=F32)

    @pl.when(j == pl.num_programs(1) - 1)
    def _():
        o_ref[...] = h_ref[...] + _rms(o_ref[...], npost_ref[...])


def _ffn(z, h, w_g, w_u, w_d, n_post):
    t, d = z.shape
    f = w_g.shape[1]
    tm = _tile(t, 1024)
    tf = _tile(f, 512)
    return pl.pallas_call(
        _ffn_kernel,
        grid=(t // tm, f // tf),
        in_specs=[pl.BlockSpec((tm, d), lambda i, j: (i, 0)),
                  pl.BlockSpec((tm, d), lambda i, j: (i, 0), pipeline_mode=pl.Buffered(1)),
                  pl.BlockSpec((d, tf), lambda i, j: (0, j)),
                  pl.BlockSpec((d, tf), lambda i, j: (0, j)),
                  pl.BlockSpec((tf, d), lambda i, j: (j, 0)),
                  pl.BlockSpec((1, d), lambda i, j: (0, 0))],
        out_specs=pl.BlockSpec((tm, d), lambda i, j: (i, 0)),
        out_shape=jax.ShapeDtypeStruct((t, d), F32),
        compiler_params=_params("parallel", "arbitrary"),
        name="ffn",
    )(z, h, w_g, w_u, w_d, n_post)


def _lambda_init(layer_idx):
    return 0.8 - 0.6 * math.exp(-0.3 * layer_idx)


def kernel(x, w_in, lambda_q1, lambda_k1, lambda_q2, lambda_k2, subln_w, ssm_a_re, ssm_a_im, ssm_log_dt, ssm_b_re, ssm_b_im, ssm_c_re, ssm_c_im, ssm_d, w_glu, b_glu, w_attn_branch, w_ssm_branch, w_out, norm_mix_pre, norm_mix_post, w_ffn_gate, w_ffn_up, w_ffn_down, norm_ffn_pre, norm_ffn_post):
    bsz, seq, d = x.shape
    depth = w_in.shape[0]
    dh = lambda_q1.shape[-1]
    e = 2 * dh
    attn_w = w_attn_branch.shape[1]
    ssm_w = w_ssm_branch.shape[1]
    n_heads = attn_w // e
    jb = ssm_w // LANES
    tc = SSM_CHUNK
    nc = seq // tc
    seg_rows = nc // SCAN_ROWS
    t = bsz * seq
    row = lambda v: v.reshape(1, -1).astype(F32)

    tm = _tile(seq, 512)
    qscale = dh ** -0.5 * math.log2(math.e)

    h = x.reshape(t, d)
    for l in range(depth):
        lam_init = _lambda_init(l)
        wq, wk, wv, wsi, wga, wgs = jnp.split(
            w_in[l], [attn_w, 2 * attn_w, 3 * attn_w, 3 * attn_w + ssm_w, 3 * attn_w + ssm_w + d],
            axis=1)
        w_main = jnp.concatenate([wq * qscale, wk, wga, wgs], axis=1).astype(BF16)
        proj, vt3, u = _in_proj(h, row(norm_mix_pre[l]), w_main, wv.T.astype(BF16),
                                wsi.astype(BF16), tm=tm, tn=d, e=e)
        y_a = _attention(proj.reshape(bsz, seq, -1), vt3, row(lambda_q1[l]), row(lambda_k1[l]),
                         row(lambda_q2[l]), row(lambda_k2[l]),
                         subln_w[l].reshape(e, 1).astype(F32),
                         n_heads=n_heads, dh=dh, lam_init=lam_init, tq=tm)
        y_a = y_a.reshape(t, attn_w)

        tables = _ssm_tables(ssm_a_re[l], ssm_a_im[l], ssm_log_dt[l], ssm_b_re[l],
                             ssm_b_im[l], ssm_c_re[l], ssm_c_im[l], ssm_d[l],
                             seg_rows=seg_rows)
        y_s = _ssm(u, tables, bsz=bsz, seg_rows=seg_rows)
        y_s = y_s.reshape(t, ssm_w)

        h1, z = _merge(h, y_a, y_s, proj, w_glu[l].astype(BF16), row(b_glu[l]),
                       w_attn_branch[l].astype(BF16), w_ssm_branch[l].astype(BF16),
                       w_out[l].astype(BF16), row(norm_mix_post[l]), row(norm_ffn_pre[l]),
                       ga_blk=2 * attn_w // d)
        h = _ffn(z, h1, w_ffn_gate[l].astype(BF16), w_ffn_up[l].astype(BF16),
                 w_ffn_down[l].astype(BF16), row(norm_ffn_post[l]))
    return h.reshape(bsz, seq, d)
```

```python
import functools
import math

import jax
import jax.numpy as jnp
from jax import lax
from jax.experimental import pallas as pl
from jax.experimental.pallas import tpu as pltpu

EPS = 1e-6
NEG_INF = -1e30
LANES = 128
SSM_CHUNK = 8
SCAN_ROWS = 8
ONES_ROWS = 16
VMEM_LIMIT_BYTES = 56 * 1024 * 1024
F32 = jnp.float32
BF16 = jnp.bfloat16
_NT = (((1,), (1,)), ((), ()))


def _params(*sem):
    return pltpu.CompilerParams(dimension_semantics=sem,
                                vmem_limit_bytes=VMEM_LIMIT_BYTES)


def _rms(x, gain):
    ms = jnp.mean(x * x, axis=-1, keepdims=True)
    return x * lax.rsqrt(ms + EPS) * gain


def _tile(n, pref):
    t = min(n, pref)
    assert n % t == 0, (n, t)
    return t


def _in_proj_kernel(x_ref, g_ref, w_ref, wvt_ref, ws_ref, o_ref, vt_ref, uc_ref,
                    u_ref, s_ref, *, e):
    @pl.when(pl.program_id(1) == 0)
    def _():
        u = _rms(x_ref[...], g_ref[...]).astype(u_ref.dtype)
        u_ref[...] = u
        vt = lax.dot_general(wvt_ref[...], u, _NT, preferred_element_type=F32)
        ea = e + ONES_ROWS
        for hd in range(vt.shape[0] // e):
            vt_ref[hd * ea:hd * ea + e, :] = vt[hd * e:(hd + 1) * e, :].astype(vt_ref.dtype)
            vt_ref[hd * ea + e:(hd + 1) * ea, :] = jnp.ones((ONES_ROWS, vt.shape[1]), vt_ref.dtype)
        s = jnp.dot(u, ws_ref[...], preferred_element_type=F32)
        nrow = uc_ref.shape[1]
        for j in range(s_ref.shape[0]):
            s_ref[j] = s[:, j * LANES:(j + 1) * LANES]
        for j in range(s_ref.shape[0]):
            for k in range(SSM_CHUNK):
                uc_ref[j, :, k * LANES:(k + 1) * LANES] = (
                    s_ref.at[j][pl.ds(k, nrow, stride=SSM_CHUNK), :].astype(uc_ref.dtype))

    o_ref[...] = jnp.dot(u_ref[...], w_ref[...],
                         preferred_element_type=F32).astype(o_ref.dtype)


def _in_proj(x2, gain, w, wvt, w_s, *, tm, tn, e):
    t, d = x2.shape
    n = w.shape[1]
    nva = wvt.shape[0] // e * (e + ONES_ROWS)
    jb = w_s.shape[1] // LANES
    kern = functools.partial(_in_proj_kernel, e=e)
    const = lambda a: pl.BlockSpec(a.shape, lambda i, j: (0, 0), pipeline_mode=pl.Buffered(1))
    return pl.pallas_call(
        kern,
        grid=(t // tm, n // tn),
        in_specs=[pl.BlockSpec((tm, d), lambda i, j: (i, 0)),
                  pl.BlockSpec((1, d), lambda i, j: (0, 0)),
                  pl.BlockSpec((d, tn), lambda i, j: (0, j)),
                  const(wvt), const(w_s)],
        out_specs=[pl.BlockSpec((tm, tn), lambda i, j: (i, j)),
                   pl.BlockSpec((None, nva, tm), lambda i, j: (i, 0, 0)),
                   pl.BlockSpec((jb, tm // SSM_CHUNK, SSM_CHUNK * LANES), lambda i, j: (0, i, 0))],
        out_shape=[jax.ShapeDtypeStruct((t, n), BF16),
                   jax.ShapeDtypeStruct((t // tm, nva, tm), BF16),
                   jax.ShapeDtypeStruct((jb, t // SSM_CHUNK, SSM_CHUNK * LANES), BF16)],
        scratch_shapes=[pltpu.VMEM((tm, d), BF16),
                        pltpu.VMEM((jb, tm, LANES), F32)],
        compiler_params=_params("parallel", "arbitrary"),
        name="in_proj",
    )(x2, gain, w, wvt, w_s)


def _attn_kernel(q_ref, qn_ref, k_ref, vt_ref, lq1_ref, lk1_ref, lq2_ref, lk2_ref, sw_ref,
                 o_ref, qs_ref, qns_ref, m_ref, acc_ref, sa_ref, sb_ref, sc_ref,
                 *, tq, tk, dh, lam_init):
    qi = pl.program_id(2)

    def stack(src_ref, dst_ref):
        qt = src_ref[...].astype(F32).T
        row = lax.broadcasted_iota(jnp.int32, qt.shape, 0)
        zero = jnp.zeros_like(qt)
        dst_ref[:, 0:tq] = jnp.where(row < dh, qt, zero).astype(dst_ref.dtype)
        dst_ref[:, tq:2 * tq] = jnp.where(row >= dh, qt, zero).astype(dst_ref.dtype)

    stack(q_ref, qs_ref)
    stack(qn_ref, qns_ref)
    m_ref[...] = jnp.full(m_ref.shape, NEG_INF, F32)
    acc_ref[...] = jnp.zeros(acc_ref.shape, F32)

    def scores(ki, s_ref, stacked_q_ref=qs_ref):
        k = k_ref[pl.ds(pl.multiple_of(ki * tk, tk), tk), :]
        s_ref[...] = jnp.dot(k, stacked_q_ref[...],
                             preferred_element_type=F32)

    def scores_next():
        scores(0, sc_ref, qns_ref)

    def consume(ki, s_ref, masked):
        s = s_ref[...]
        if masked:
            kpos = ki * tk + lax.broadcasted_iota(jnp.int32, s.shape, 0)
            col = lax.broadcasted_iota(jnp.int32, s.shape, 1)
            qpos = qi * tq + jnp.where(col >= tq, col - tq, col)
            s = jnp.where(kpos <= qpos, s, NEG_INF)
        m_prev = m_ref[...]
        m_new = jnp.maximum(m_prev, jnp.max(s, axis=0, keepdims=True))
        alpha = jnp.exp2(m_prev - m_new)
        p = jnp.exp2(s - m_new)
        acc_ref[...] = alpha * acc_ref[...] + jnp.dot(
            vt_ref[ki], p.astype(vt_ref.dtype), preferred_element_type=F32)
        m_ref[...] = m_new

    @pl.when(qi == 0)
    def _():
        scores(0, sa_ref)
        scores_next()
        consume(0, sa_ref, True)

    @pl.when(qi > 0)
    def _():
        scores(1, sb_ref)
        consume(0, sc_ref, False)

        def pair(t):
            scores(t + 1, sa_ref)
            consume(t, sb_ref, False)
            scores(t + 2, sb_ref)
            consume(t + 1, sa_ref, False)

        def quad(j, carry):
            pair(4 * j + 1)
            pair(4 * j + 3)
            return carry

        nquad = (qi - 1) // 4
        lax.fori_loop(0, nquad, quad, 0)

        @pl.when((qi - 1) % 4 >= 2)
        def _():
            pair(4 * nquad + 1)

        @pl.when(qi % 2 == 1)
        def _():
            scores_next()
            consume(qi, sb_ref, True)

        @pl.when(qi % 2 == 0)
        def _():
            scores(qi, sa_ref)
            consume(qi - 1, sb_ref, False)
            scores_next()
            consume(qi, sa_ref, True)

    lam = (jnp.exp(jnp.sum(lq1_ref[...] * lk1_ref[...], axis=-1, keepdims=True))
           - jnp.exp(jnp.sum(lq2_ref[...] * lk2_ref[...], axis=-1, keepdims=True))
           + lam_init)
    e = 2 * dh
    acc = acc_ref[...]
    o_all = acc[0:e, :] / acc[e:e + 1, :]
    o = o_all[:, 0:tq] - lam * o_all[:, tq:2 * tq]
    ms = jnp.mean(o * o, axis=0, keepdims=True)
    o = o * lax.rsqrt(ms + EPS) * sw_ref[...] * (1.0 - lam_init)
    o_ref[...] = o.T.astype(o_ref.dtype)


def _attention(proj3, vt3, lq1, lk1, lq2, lk2, subln, *, n_heads, dh, lam_init, tq):
    bsz, seq, _ = proj3.shape
    e = 2 * dh
    nk = seq // tq
    ea = e + ONES_ROWS
    assert vt3.shape == (bsz * nk, n_heads * ea, tq)
    kern = functools.partial(_attn_kernel, tq=tq, tk=tq, dh=dh, lam_init=lam_init)
    vec = pl.BlockSpec((1, dh), lambda b, h, i: (0, 0))
    return pl.pallas_call(
        kern,
        grid=(bsz, n_heads, nk),
        in_specs=[pl.BlockSpec((None, tq, e), lambda b, h, i: (b, i, h)),
                  pl.BlockSpec((None, tq, e), lambda b, h, i: (b, jnp.minimum(i + 1, nk - 1), h)),
                  pl.BlockSpec((None, seq, e), lambda b, h, i: (b, 0, n_heads + h)),
                  pl.BlockSpec((nk, ea, tq), lambda b, h, i: (b, h, 0)),
                  vec, vec, vec, vec,
                  pl.BlockSpec((e, 1), lambda b, h, i: (0, 0))],
        out_specs=pl.BlockSpec((None, tq, e), lambda b, h, i: (b, i, h)),
        out_shape=jax.ShapeDtypeStruct((bsz, seq, n_heads * e), BF16),
        scratch_shapes=[pltpu.VMEM((e, 2 * tq), BF16),
                        pltpu.VMEM((e, 2 * tq), BF16),
                        pltpu.VMEM((1, 2 * tq), F32),
                        pltpu.VMEM((ea, 2 * tq), F32),
                        pltpu.VMEM((tq, 2 * tq), F32),
                        pltpu.VMEM((tq, 2 * tq), F32),
                        pltpu.VMEM((tq, 2 * tq), F32)],
        compiler_params=_params("parallel", "parallel", "arbitrary"),
        name="diff_attention",
    )(proj3, proj3, proj3, vt3, lq1, lk1, lq2, lk2, subln)


def _ssm_tables(a_re, a_im, log_dt, b_re, b_im, c_re, c_im, d_skip, *, seg_rows):
    g, p, h = b_re.shape
    tc = SSM_CHUNK
    gpb = LANES // h
    jb = g // gpb
    dt = jnp.exp(log_dt)[:, None]
    er, ei = a_re * dt, a_im * dt

    def powers(n):
        n = jnp.asarray(n, F32).reshape((-1, 1, 1))
        mag = jnp.exp(n * er)
        return mag * jnp.cos(n * ei), mag * jnp.sin(n * ei)

    ar, ai = powers([1.0])
    ar, ai = ar[0], ai[0]
    den = a_re * a_re + a_im * a_im
    fr = ((ar - 1.0) * a_re + ai * a_im) / den
    fi = (ai * a_re - (ar - 1.0) * a_im) / den
    bbr = fr[..., None] * b_re - fi[..., None] * b_im
    bbi = fr[..., None] * b_im + fi[..., None] * b_re

    pr, pi = powers(jnp.arange(tc + 1))
    cpr = c_re[None] * pr[:, :, None, :] - c_im[None] * pi[:, :, None, :]
    cpi = c_re[None] * pi[:, :, None, :] + c_im[None] * pr[:, :, None, :]
    kern = (jnp.einsum('tgop,gpi->tgio', cpr[:tc], bbr)
            - jnp.einsum('tgop,gpi->tgio', cpi[:tc], bbi))
    def group_mask(rows_per_group, cols_per_group):
        r = jnp.arange(gpb * rows_per_group) // rows_per_group
        c = jnp.arange(gpb * cols_per_group) // cols_per_group
        return r[:, None] == c[None, :]

    def block_diag(x, rows_per_group, cols_per_group):
        x = jnp.tile(x, (1,) * (x.ndim - 1) + (gpb,))
        return jnp.where(group_mask(rows_per_group, cols_per_group), x, 0.0)

    kidx = jnp.arange(tc)
    tau = kidx[None, :] - kidx[:, None]
    bd = block_diag(kern.reshape(tc, jb, LANES, h), h, h)
    kt = jnp.where((tau >= 0)[:, :, None, None, None], bd[jnp.clip(tau, 0)], 0.0)
    t_mat = jnp.transpose(kt, (2, 0, 3, 1, 4)).reshape(jb, tc * LANES, tc * LANES)

    qr, qi_ = pr[tc - 1 - kidx], pi[tc - 1 - kidx]
    sr = qr[:, :, None, :] * jnp.swapaxes(bbr, 1, 2)[None] - qi_[:, :, None, :] * jnp.swapaxes(bbi, 1, 2)[None]
    si = qr[:, :, None, :] * jnp.swapaxes(bbi, 1, 2)[None] + qi_[:, :, None, :] * jnp.swapaxes(bbr, 1, 2)[None]
    st = jnp.stack([sr, si], 0).reshape(2, tc, jb, LANES, p)
    st = block_diag(st, h, p)
    w_st = jnp.transpose(st, (2, 1, 3, 0, 4)).reshape(jb, tc * LANES, 2 * gpb * p)

    ot = jnp.stack([cpr[1:], -cpi[1:]], 0)
    ot = jnp.swapaxes(ot, 3, 4).reshape(2, tc, jb, gpb * p, h)
    ot = block_diag(ot, p, h)
    w_out = jnp.transpose(ot, (2, 0, 3, 1, 4)).reshape(jb, 2 * gpb * p, tc * LANES)

    def rows(n):
        xr, xi = powers(n)
        x = jnp.stack([xr, xi], 1).reshape(-1, 2, jb, gpb * p)
        return jnp.transpose(x, (2, 0, 1, 3)).reshape(jb, -1, 2 * gpb * p)

    a_chunk = rows([float(tc)])
    a_seg = rows([float(tc * seg_rows)])
    p_pow = rows(tc * jnp.arange(seg_rows))
    d_row = jnp.tile(d_skip.reshape(jb, 1, LANES), (1, 1, tc))
    return (t_mat.astype(BF16), w_st.astype(BF16), w_out.astype(BF16),
            a_chunk, a_seg, p_pow, d_row)


def _cmul(ar, ai, xr, xi):
    return ar * xr - ai * xi, ar * xi + ai * xr


def _ssm_kernel(u_ref, t_ref, wst_ref, wout_ref, ach_ref, aseg_ref, pp_ref, d_ref,
                y_ref, xst_ref, e_ref, sp_ref, z_ref, *, seg_rows):
    ncb = xst_ref.shape[0]
    hb = ncb // 2
    pitch = xst_ref.shape[1] // SCAN_ROWS
    blk = lambda c: slice(c * LANES, (c + 1) * LANES)
    u = u_ref[...]
    xst = jnp.dot(u, wst_ref[...], preferred_element_type=F32)
    for c in range(ncb):
        for s in range(SCAN_ROWS):
            xst_ref[c, s * pitch:s * pitch + seg_rows, :] = (
                xst[s * seg_rows:(s + 1) * seg_rows, blk(c)])

    ach = ach_ref[...]
    a_ch = [jnp.broadcast_to(ach[:, blk(c)], (SCAN_ROWS, LANES)) for c in range(ncb)]

    def cstep(a, e, c):
        return (a[c] * e[c] - a[c + hb] * e[c + hb], a[c] * e[c + hb] + a[c + hb] * e[c])

    def scan_body(i, e):
        row0 = pl.multiple_of(i * SCAN_ROWS, SCAN_ROWS)
        new = [None] * ncb
        for c in range(hb):
            e_ref[c, pl.ds(row0, SCAN_ROWS), :] = e[c]
            e_ref[c + hb, pl.ds(row0, SCAN_ROWS), :] = e[c + hb]
            nr, ni = cstep(a_ch, e, c)
            new[c] = nr + xst_ref.at[c][pl.ds(i, SCAN_ROWS, stride=pitch), :]
            new[c + hb] = ni + xst_ref.at[c + hb][pl.ds(i, SCAN_ROWS, stride=pitch), :]
        return tuple(new)

    zero = jnp.zeros((SCAN_ROWS, LANES), F32)
    end = lax.fori_loop(0, seg_rows, scan_body, (zero,) * ncb, unroll=4)

    asg = aseg_ref[...]
    a_sg = [asg[:, blk(c)] for c in range(ncb)]
    car = [jnp.zeros((1, LANES), F32)] * ncb
    pp = pp_ref[...]
    p_pw = [pp[:, blk(c)] for c in range(ncb)]
    for s in range(SCAN_ROWS):
        rows = slice(s * seg_rows, (s + 1) * seg_rows)
        nxt = [None] * ncb
        for c in range(hb):
            fr, fi = cstep(p_pw, car, c)
            sp_ref[rows, blk(c)] = (
                e_ref.at[c][pl.ds(s, seg_rows, stride=SCAN_ROWS), :] + fr).astype(sp_ref.dtype)
            sp_ref[rows, blk(c + hb)] = (
                e_ref.at[c + hb][pl.ds(s, seg_rows, stride=SCAN_ROWS), :] + fi).astype(sp_ref.dtype)
            nr, ni = cstep(a_sg, car, c)
            nxt[c] = nr + end[c][s:s + 1, :]
            nxt[c + hb] = ni + end[c + hb][s:s + 1, :]
        car = nxt

    y = jnp.dot(u, t_ref[...], preferred_element_type=F32)
    y = y + jnp.dot(sp_ref[...], wout_ref[...], preferred_element_type=F32)
    y = y + d_ref[...] * u.astype(F32)

    tc = SSM_CHUNK
    ng = y.shape[0] // SCAN_ROWS
    for k in range(tc):
        z_ref[:, k * SCAN_ROWS:(k + 1) * SCAN_ROWS, :] = (
            y[:, blk(k)].reshape(ng, SCAN_ROWS, LANES))
    for c in range(SCAN_ROWS):
        y_ref[:, c * tc:(c + 1) * tc, :] = z_ref[:, pl.ds(c, tc, stride=SCAN_ROWS), :]


def _ssm(u, tables, *, bsz, seg_rows):
    t_mat, w_st, w_out, a_chunk, a_seg, p_pow, d_row = tables
    jb, m, kc = u.shape
    nc = m // bsz
    s2 = w_st.shape[2]
    assert nc == SCAN_ROWS * seg_rows
    kern = functools.partial(_ssm_kernel, seg_rows=seg_rows)
    wspec = lambda r, c: pl.BlockSpec((None, r, c), lambda j, b: (j, 0, 0))
    return pl.pallas_call(
        kern,
        grid=(jb, bsz),
        in_specs=[pl.BlockSpec((None, nc, kc), lambda j, b: (j, b, 0)),
                  wspec(kc, kc), wspec(kc, s2), wspec(s2, kc),
                  wspec(1, s2), wspec(1, s2), wspec(seg_rows, s2), wspec(1, kc)],
        out_specs=pl.BlockSpec((nc // SCAN_ROWS, SCAN_ROWS * SSM_CHUNK, LANES),
                               lambda j, b: (b, 0, j)),
        out_shape=jax.ShapeDtypeStruct((m // SCAN_ROWS, SCAN_ROWS * SSM_CHUNK, jb * LANES), F32),
        scratch_shapes=[pltpu.VMEM((s2 // LANES, SCAN_ROWS * (seg_rows + 8), LANES), F32),
                        pltpu.VMEM((s2 // LANES, nc, LANES), F32),
                        pltpu.VMEM((nc, s2), BF16),
                        pltpu.VMEM((nc // SCAN_ROWS, SCAN_ROWS * SSM_CHUNK, LANES), F32)],
        compiler_params=_params("parallel", "arbitrary"),
        name="s5_scan",
    )(u, t_mat, w_st, w_out, a_chunk, a_seg, p_pow, d_row)


def _merge_kernel(x_ref, ya_ref, ys_ref, ga_ref, gs_ref, wglu_ref, bglu_ref, wa_ref,
                  ws_ref, wo_ref, npost_ref, npre_ref, h_ref, z_ref):
    ys = jax.nn.gelu(ys_ref[...].astype(F32), approximate=True)
    gate = jnp.dot(ys.astype(BF16), wglu_ref[...], preferred_element_type=F32)
    ys = ys * jax.nn.sigmoid(gate + bglu_ref[...])
    ma = jnp.dot(ya_ref[...], wa_ref[...], preferred_element_type=F32)
    ms = jnp.dot(ys.astype(BF16), ws_ref[...], preferred_element_type=F32)
    merged = (jax.nn.sigmoid(ga_ref[...].astype(F32)) * ma
              + jax.nn.sigmoid(gs_ref[...].astype(F32)) * ms)
    mix = jnp.dot(merged.astype(BF16), wo_ref[...], preferred_element_type=F32)
    h = x_ref[...] + _rms(mix, npost_ref[...])
    h_ref[...] = h
    z_ref[...] = _rms(h, npre_ref[...]).astype(z_ref.dtype)


def _merge(x2, y_a, y_s, proj, w_glu, b_glu, w_a, w_s, w_o, n_post, n_pre, *, ga_blk):
    t, d = x2.shape
    wa = y_a.shape[1]
    wsm = y_s.shape[1]
    tm = _tile(t, 256)
    row = lambda c: pl.BlockSpec((tm, c), lambda i: (i, 0))
    full = lambda a: pl.BlockSpec(a.shape, lambda i: (0, 0), pipeline_mode=pl.Buffered(1))
    return pl.pallas_call(
        _merge_kernel,
        grid=(t // tm,),
        in_specs=[row(d), row(wa), row(wsm),
                  pl.BlockSpec((tm, d), lambda i: (i, ga_blk)),
                  pl.BlockSpec((tm, d), lambda i: (i, ga_blk + 1)),
                  full(w_glu), full(b_glu), full(w_a), full(w_s), full(w_o),
                  full(n_post), full(n_pre)],
        out_specs=[row(d), row(d)],
        out_shape=[jax.ShapeDtypeStruct((t, d), F32),
                   jax.ShapeDtypeStruct((t, d), BF16)],
        compiler_params=_params("parallel"),
        name="merge",
    )(x2, y_a, y_s, proj, proj, w_glu, b_glu, w_a, w_s, w_o, n_post, n_pre)


def _ffn_kernel(z_ref, h_ref, wg_ref, wu_ref, wd_ref, npost_ref, o_ref):
    j = pl.program_id(1)

    @pl.when(j == 0)
    def _():
        o_ref[...] = jnp.zeros(o_ref.shape, o_ref.dtype)

    z = z_ref[...]
    g = jnp.dot(z, wg_ref[...], preferred_element_type=F32)
    u = jnp.dot(z, wu_ref[...], preferred_element_type=F32)
    a = (jax.nn.silu(g) * u).astype(BF16)
    o_ref[...] += jnp.dot(a, wd_ref[...], preferred_element_type=F32)

    @pl.when(j == pl.num_programs(1) - 1)
    def _():
        o_ref[...] = h_ref[...] + _rms(o_ref[...], npost_ref[...])


def _ffn(z, h, w_g, w_u, w_d, n_post):
    t, d = z.shape
    f = w_g.shape[1]
    tm = _tile(t, 512)
    tf = _tile(f, 512)
    return pl.pallas_call(
        _ffn_kernel,
        grid=(t // tm, f // tf),
        in_specs=[pl.BlockSpec((tm, d), lambda i, j: (i, 0)),
                  pl.BlockSpec((tm, d), lambda i, j: (i, 0)),
                  pl.BlockSpec((d, tf), lambda i, j: (0, j)),
                  pl.BlockSpec((d, tf), lambda i, j: (0, j)),
                  pl.BlockSpec((tf, d), lambda i, j: (j, 0)),
                  pl.BlockSpec((1, d), lambda i, j: (0, 0))],
        out_specs=pl.BlockSpec((tm, d), lambda i, j: (i, 0)),
        out_shape=jax.ShapeDtypeStruct((t, d), F32),
        compiler_params=_params("parallel", "arbitrary"),
        name="ffn",
    )(z, h, w_g, w_u, w_d, n_post)


def _lambda_init(layer_idx):
    return 0.8 - 0.6 * math.exp(-0.3 * layer_idx)


def kernel(x, w_in, lambda_q1, lambda_k1, lambda_q2, lambda_k2, subln_w, ssm_a_re, ssm_a_im, ssm_log_dt, ssm_b_re, ssm_b_im, ssm_c_re, ssm_c_im, ssm_d, w_glu, b_glu, w_attn_branch, w_ssm_branch, w_out, norm_mix_pre, norm_mix_post, w_ffn_gate, w_ffn_up, w_ffn_down, norm_ffn_pre, norm_ffn_post):
    bsz, seq, d = x.shape
    depth = w_in.shape[0]
    dh = lambda_q1.shape[-1]
    e = 2 * dh
    attn_w = w_attn_branch.shape[1]
    ssm_w = w_ssm_branch.shape[1]
    n_heads = attn_w // e
    jb = ssm_w // LANES
    tc = SSM_CHUNK
    nc = seq // tc
    seg_rows = nc // SCAN_ROWS
    t = bsz * seq
    row = lambda v: v.reshape(1, -1).astype(F32)

    tm = _tile(seq, 512)
    qscale = dh ** -0.5 * math.log2(math.e)

    h = x.reshape(t, d)
    for l in range(depth):
        lam_init = _lambda_init(l)
        wq, wk, wv, wsi, wga, wgs = jnp.split(
            w_in[l], [attn_w, 2 * attn_w, 3 * attn_w, 3 * attn_w + ssm_w, 3 * attn_w + ssm_w + d],
            axis=1)
        w_main = jnp.concatenate([wq * qscale, wk, wga, wgs], axis=1).astype(BF16)
        proj, vt3, u = _in_proj(h, row(norm_mix_pre[l]), w_main, wv.T.astype(BF16),
                                wsi.astype(BF16), tm=tm, tn=d, e=e)
        y_a = _attention(proj.reshape(bsz, seq, -1), vt3, row(lambda_q1[l]), row(lambda_k1[l]),
                         row(lambda_q2[l]), row(lambda_k2[l]),
                         subln_w[l].reshape(e, 1).astype(F32),
                         n_heads=n_heads, dh=dh, lam_init=lam_init, tq=tm)
        y_a = y_a.reshape(t, attn_w)

        tables = _ssm_tables(ssm_a_re[l], ssm_a_im[l], ssm_log_dt[l], ssm_b_re[l],
                             ssm_b_im[l], ssm_c_re[l], ssm_c_im[l], ssm_d[l],
                             seg_rows=seg_rows)
        y_s = _ssm(u, tables, bsz=bsz, seg_rows=seg_rows)
        y_s = y_s.reshape(t, ssm_w)

        h1, z = _merge(h, y_a, y_s, proj, w_glu[l].astype(BF16), row(b_glu[l]),
                       w_attn_branch[l].astype(BF16), w_ssm_branch[l].astype(BF16),
                       w_out[l].astype(BF16), row(norm_mix_post[l]), row(norm_ffn_pre[l]),
                       ga_blk=2 * attn_w // d)
        h = _ffn(z, h1, w_ffn_gate[l].astype(BF16), w_ffn_up[l].astype(BF16),
                 w_ffn_down[l].astype(BF16), row(norm_ffn_post[l]))
    return h.reshape(bsz, seq, d)
```

```python
import functools
import math

import jax
import jax.numpy as jnp
from jax import lax
from jax.experimental import pallas as pl
from jax.experimental.pallas import tpu as pltpu

EPS = 1e-6
NEG_INF = -1e30
LANES = 128
SSM_CHUNK = 8
SCAN_ROWS = 8
ONES_ROWS = 16
VMEM_LIMIT_BYTES = 56 * 1024 * 1024
F32 = jnp.float32
BF16 = jnp.bfloat16
_NT = (((1,), (1,)), ((), ()))


def _params(*sem):
    return pltpu.CompilerParams(dimension_semantics=sem,
                                vmem_limit_bytes=VMEM_LIMIT_BYTES)


def _rms(x, gain):
    ms = jnp.mean(x * x, axis=-1, keepdims=True)
    return x * lax.rsqrt(ms + EPS) * gain


def _tile(n, pref):
    t = min(n, pref)
    assert n % t == 0, (n, t)
    return t


def _in_proj_kernel(x_ref, g_ref, w_ref, wvt_ref, ws_ref, o_ref, vt_ref, uc_ref,
                    u_ref, s_ref, *, e):
    @pl.when(pl.program_id(1) == 0)
    def _():
        u = _rms(x_ref[...], g_ref[...]).astype(u_ref.dtype)
        u_ref[...] = u
        vt = lax.dot_general(wvt_ref[...], u, _NT, preferred_element_type=F32)
        ea = e + ONES_ROWS
        for hd in range(vt.shape[0] // e):
            vt_ref[hd * ea:hd * ea + e, :] = vt[hd * e:(hd + 1) * e, :].astype(vt_ref.dtype)
            vt_ref[hd * ea + e:(hd + 1) * ea, :] = jnp.ones((ONES_ROWS, vt.shape[1]), vt_ref.dtype)
        s = jnp.dot(u, ws_ref[...], preferred_element_type=F32)
        nrow = uc_ref.shape[1]
        for j in range(s_ref.shape[0]):
            s_ref[j] = s[:, j * LANES:(j + 1) * LANES]
        for j in range(s_ref.shape[0]):
            for k in range(SSM_CHUNK):
                uc_ref[j, :, k * LANES:(k + 1) * LANES] = (
                    s_ref.at[j][pl.ds(k, nrow, stride=SSM_CHUNK), :].astype(uc_ref.dtype))

    o_ref[...] = jnp.dot(u_ref[...], w_ref[...],
                         preferred_element_type=F32).astype(o_ref.dtype)


def _in_proj(x2, gain, w, wvt, w_s, *, tm, tn, e):
    t, d = x2.shape
    n = w.shape[1]
    nva = wvt.shape[0] // e * (e + ONES_ROWS)
    jb = w_s.shape[1] // LANES
    kern = functools.partial(_in_proj_kernel, e=e)
    const = lambda a: pl.BlockSpec(a.shape, lambda i, j: (0, 0), pipeline_mode=pl.Buffered(1))
    return pl.pallas_call(
        kern,
        grid=(t // tm, n // tn),
        in_specs=[pl.BlockSpec((tm, d), lambda i, j: (i, 0)),
                  pl.BlockSpec((1, d), lambda i, j: (0, 0)),
                  pl.BlockSpec((d, tn), lambda i, j: (0, j)),
                  const(wvt), const(w_s)],
        out_specs=[pl.BlockSpec((tm, tn), lambda i, j: (i, j)),
                   pl.BlockSpec((None, nva, tm), lambda i, j: (i, 0, 0)),
                   pl.BlockSpec((jb, tm // SSM_CHUNK, SSM_CHUNK * LANES), lambda i, j: (0, i, 0))],
        out_shape=[jax.ShapeDtypeStruct((t, n), BF16),
                   jax.ShapeDtypeStruct((t // tm, nva, tm), BF16),
                   jax.ShapeDtypeStruct((jb, t // SSM_CHUNK, SSM_CHUNK * LANES), BF16)],
        scratch_shapes=[pltpu.VMEM((tm, d), BF16),
                        pltpu.VMEM((jb, tm, LANES), F32)],
        compiler_params=_params("parallel", "arbitrary"),
        name="in_proj",
    )(x2, gain, w, wvt, w_s)


def _attn_kernel(q_ref, k_ref, vt_ref, lq1_ref, lk1_ref, lq2_ref, lk2_ref, sw_ref,
                 o_ref, qs_ref, qns_ref, m_ref, acc_ref, sa_ref, sb_ref, sc_ref,
                 *, tq, tk, dh, lam_init):
    nq = q_ref.shape[0] // tq
    lax.fori_loop(0, nq, functools.partial(
        _attn_q_tile, q_ref, k_ref, vt_ref, lq1_ref, lk1_ref, lq2_ref, lk2_ref, sw_ref,
        o_ref, qs_ref, qns_ref, m_ref, acc_ref, sa_ref, sb_ref, sc_ref,
        nq=nq, tq=tq, tk=tk, dh=dh, lam_init=lam_init), 0)


def _attn_q_tile(q_ref, k_ref, vt_ref, lq1_ref, lk1_ref, lq2_ref, lk2_ref, sw_ref,
                 o_ref, qs_ref, qns_ref, m_ref, acc_ref, sa_ref, sb_ref, sc_ref,
                 qi, carry, *, nq, tq, tk, dh, lam_init):
    def stack(tile, dst_ref):
        rows = pl.ds(pl.multiple_of(tile * tq, tq), tq)
        qt = q_ref[rows, :].astype(F32).T
        row = lax.broadcasted_iota(jnp.int32, qt.shape, 0)
        zero = jnp.zeros_like(qt)
        dst_ref[:, 0:tq] = jnp.where(row < dh, qt, zero).astype(dst_ref.dtype)
        dst_ref[:, tq:2 * tq] = jnp.where(row >= dh, qt, zero).astype(dst_ref.dtype)

    stack(qi, qs_ref)
    stack(jnp.minimum(qi + 1, nq - 1), qns_ref)
    m_ref[...] = jnp.full(m_ref.shape, NEG_INF, F32)
    acc_ref[...] = jnp.zeros(acc_ref.shape, F32)

    def scores(ki, s_ref, stacked_q_ref=qs_ref):
        k = k_ref[pl.ds(pl.multiple_of(ki * tk, tk), tk), :]
        s_ref[...] = jnp.dot(k, stacked_q_ref[...],
                             preferred_element_type=F32)

    def scores_next():
        scores(0, sc_ref, qns_ref)

    def consume(ki, s_ref, masked):
        s = s_ref[...]
        if masked:
            kpos = ki * tk + lax.broadcasted_iota(jnp.int32, s.shape, 0)
            col = lax.broadcasted_iota(jnp.int32, s.shape, 1)
            qpos = qi * tq + jnp.where(col >= tq, col - tq, col)
            s = jnp.where(kpos <= qpos, s, NEG_INF)
        m_prev = m_ref[...]
        m_new = jnp.maximum(m_prev, jnp.max(s, axis=0, keepdims=True))
        alpha = jnp.exp2(m_prev - m_new)
        p = jnp.exp2(s - m_new)
        acc_ref[...] = alpha * acc_ref[...] + jnp.dot(
            vt_ref[ki], p.astype(vt_ref.dtype), preferred_element_type=F32)
        m_ref[...] = m_new

    @pl.when(qi == 0)
    def _():
        scores(0, sa_ref)
        scores_next()
        consume(0, sa_ref, True)

    @pl.when(qi > 0)
    def _():
        scores(1, sb_ref)
        consume(0, sc_ref, False)

        def pair(t):
            scores(t + 1, sa_ref)
            consume(t, sb_ref, False)
            scores(t + 2, sb_ref)
            consume(t + 1, sa_ref, False)

        def quad(j, carry):
            pair(4 * j + 1)
            pair(4 * j + 3)
            return carry

        nquad = (qi - 1) // 4
        lax.fori_loop(0, nquad, quad, 0)

        @pl.when((qi - 1) % 4 >= 2)
        def _():
            pair(4 * nquad + 1)

        @pl.when(qi % 2 == 1)
        def _():
            scores_next()
            consume(qi, sb_ref, True)

        @pl.when(qi % 2 == 0)
        def _():
            scores(qi, sa_ref)
            consume(qi - 1, sb_ref, False)
            scores_next()
            consume(qi, sa_ref, True)

    lam = (jnp.exp(jnp.sum(lq1_ref[...] * lk1_ref[...], axis=-1, keepdims=True))
           - jnp.exp(jnp.sum(lq2_ref[...] * lk2_ref[...], axis=-1, keepdims=True))
           + lam_init)
    e = 2 * dh
    acc = acc_ref[...]
    o_all = acc[0:e, :] / acc[e:e + 1, :]
    o = o_all[:, 0:tq] - lam * o_all[:, tq:2 * tq]
    ms = jnp.mean(o * o, axis=0, keepdims=True)
    o = o * lax.rsqrt(ms + EPS) * sw_ref[...] * (1.0 - lam_init)
    o_ref[pl.ds(pl.multiple_of(qi * tq, tq), tq), :] = o.T.astype(o_ref.dtype)
    return carry


def _attention(proj3, vt3, lq1, lk1, lq2, lk2, subln, *, n_heads, dh, lam_init, tq):
    bsz, seq, _ = proj3.shape
    e = 2 * dh
    nk = seq // tq
    ea = e + ONES_ROWS
    assert vt3.shape == (bsz * nk, n_heads * ea, tq)
    kern = functools.partial(_attn_kernel, tq=tq, tk=tq, dh=dh, lam_init=lam_init)
    vec = pl.BlockSpec((1, dh), lambda b, h: (0, 0))
    return pl.pallas_call(
        kern,
        grid=(bsz, n_heads),
        in_specs=[pl.BlockSpec((None, seq, e), lambda b, h: (b, 0, h)),
                  pl.BlockSpec((None, seq, e), lambda b, h: (b, 0, n_heads + h)),
                  pl.BlockSpec((nk, ea, tq), lambda b, h: (b, h, 0)),
                  vec, vec, vec, vec,
                  pl.BlockSpec((e, 1), lambda b, h: (0, 0))],
        out_specs=pl.BlockSpec((None, seq, e), lambda b, h: (b, 0, h)),
        out_shape=jax.ShapeDtypeStruct((bsz, seq, n_heads * e), BF16),
        scratch_shapes=[pltpu.VMEM((e, 2 * tq), BF16),
                        pltpu.VMEM((e, 2 * tq), BF16),
                        pltpu.VMEM((1, 2 * tq), F32),
                        pltpu.VMEM((ea, 2 * tq), F32),
                        pltpu.VMEM((tq, 2 * tq), F32),
                        pltpu.VMEM((tq, 2 * tq), F32),
                        pltpu.VMEM((tq, 2 * tq), F32)],
        compiler_params=_params("parallel", "parallel"),
        name="diff_attention",
    )(proj3, proj3, vt3, lq1, lk1, lq2, lk2, subln)


def _ssm_tables(a_re, a_im, log_dt, b_re, b_im, c_re, c_im, d_skip, *, seg_rows):
    g, p, h = b_re.shape
    tc = SSM_CHUNK
    gpb = LANES // h
    jb = g // gpb
    dt = jnp.exp(log_dt)[:, None]
    er, ei = a_re * dt, a_im * dt

    def powers(n):
        n = jnp.asarray(n, F32).reshape((-1, 1, 1))
        mag = jnp.exp(n * er)
        return mag * jnp.cos(n * ei), mag * jnp.sin(n * ei)

    ar, ai = powers([1.0])
    ar, ai = ar[0], ai[0]
    den = a_re * a_re + a_im * a_im
    fr = ((ar - 1.0) * a_re + ai * a_im) / den
    fi = (ai * a_re - (ar - 1.0) * a_im) / den
    bbr = fr[..., None] * b_re - fi[..., None] * b_im
    bbi = fr[..., None] * b_im + fi[..., None] * b_re

    pr, pi = powers(jnp.arange(tc + 1))
    cpr = c_re[None] * pr[:, :, None, :] - c_im[None] * pi[:, :, None, :]
    cpi = c_re[None] * pi[:, :, None, :] + c_im[None] * pr[:, :, None, :]
    kern = (jnp.einsum('tgop,gpi->tgio', cpr[:tc], bbr)
            - jnp.einsum('tgop,gpi->tgio', cpi[:tc], bbi))
    def group_mask(rows_per_group, cols_per_group):
        r = jnp.arange(gpb * rows_per_group) // rows_per_group
        c = jnp.arange(gpb * cols_per_group) // cols_per_group
        return r[:, None] == c[None, :]

    def block_diag(x, rows_per_group, cols_per_group):
        x = jnp.tile(x, (1,) * (x.ndim - 1) + (gpb,))
        return jnp.where(group_mask(rows_per_group, cols_per_group), x, 0.0)

    kidx = jnp.arange(tc)
    tau = kidx[None, :] - kidx[:, None]
    bd = block_diag(kern.reshape(tc, jb, LANES, h), h, h)
    kt = jnp.where((tau >= 0)[:, :, None, None, None], bd[jnp.clip(tau, 0)], 0.0)
    t_mat = jnp.transpose(kt, (2, 0, 3, 1, 4)).reshape(jb, tc * LANES, tc * LANES)

    qr, qi_ = pr[tc - 1 - kidx], pi[tc - 1 - kidx]
    sr = qr[:, :, None, :] * jnp.swapaxes(bbr, 1, 2)[None] - qi_[:, :, None, :] * jnp.swapaxes(bbi, 1, 2)[None]
    si = qr[:, :, None, :] * jnp.swapaxes(bbi, 1, 2)[None] + qi_[:, :, None, :] * jnp.swapaxes(bbr, 1, 2)[None]
    st = jnp.stack([sr, si], 0).reshape(2, tc, jb, LANES, p)
    st = block_diag(st, h, p)
    w_st = jnp.transpose(st, (2, 1, 3, 0, 4)).reshape(jb, tc * LANES, 2 * gpb * p)

    ot = jnp.stack([cpr[1:], -cpi[1:]], 0)
    ot = jnp.swapaxes(ot, 3, 4).reshape(2, tc, jb, gpb * p, h)
    ot = block_diag(ot, p, h)
    w_out = jnp.transpose(ot, (2, 0, 3, 1, 4)).reshape(jb, 2 * gpb * p, tc * LANES)

    def rows(n):
        xr, xi = powers(n)
        x = jnp.stack([xr, xi], 1).reshape(-1, 2, jb, gpb * p)
        return jnp.transpose(x, (2, 0, 1, 3)).reshape(jb, -1, 2 * gpb * p)

    a_chunk = rows([float(tc)])
    a_seg = rows([float(tc * seg_rows)])
    p_pow = rows(tc * jnp.arange(seg_rows))
    d_row = jnp.tile(d_skip.reshape(jb, 1, LANES), (1, 1, tc))
    return (t_mat.astype(BF16), w_st.astype(BF16), w_out.astype(BF16),
            a_chunk, a_seg, p_pow, d_row)


def _segment_pitch(seg_rows):
    return seg_rows + SCAN_ROWS


def _ssm_kernel(u_ref, t_ref, wst_ref, wout_ref, ach_ref, aseg_ref, pp_ref, d_ref,
                y_ref, xst_ref, e_ref, sp_ref, z_ref, *, seg_rows):
    ncb = xst_ref.shape[0]
    hb = ncb // 2
    pitch = _segment_pitch(seg_rows)
    blk = lambda c: slice(c * LANES, (c + 1) * LANES)
    u = u_ref[...]
    xst = jnp.dot(u, wst_ref[...], preferred_element_type=F32)
    for c in range(ncb):
        for s in range(SCAN_ROWS):
            xst_ref[c, s * pitch:s * pitch + seg_rows, :] = (
                xst[s * seg_rows:(s + 1) * seg_rows, blk(c)])

    ach = ach_ref[...]
    a_ch = [jnp.broadcast_to(ach[:, blk(c)], (SCAN_ROWS, LANES)) for c in range(ncb)]

    def cstep(a, e, c):
        return (a[c] * e[c] - a[c + hb] * e[c + hb], a[c] * e[c + hb] + a[c + hb] * e[c])

    def scan_body(i, e):
        row0 = pl.multiple_of(i * SCAN_ROWS, SCAN_ROWS)
        new = [None] * ncb
        for c in range(hb):
            e_ref[c, pl.ds(row0, SCAN_ROWS), :] = e[c]
            e_ref[c + hb, pl.ds(row0, SCAN_ROWS), :] = e[c + hb]
            nr, ni = cstep(a_ch, e, c)
            new[c] = nr + xst_ref.at[c][pl.ds(i, SCAN_ROWS, stride=pitch), :]
            new[c + hb] = ni + xst_ref.at[c + hb][pl.ds(i, SCAN_ROWS, stride=pitch), :]
        return tuple(new)

    zero = jnp.zeros((SCAN_ROWS, LANES), F32)
    end = lax.fori_loop(0, seg_rows, scan_body, (zero,) * ncb, unroll=4)

    asg = aseg_ref[...]
    a_sg = [asg[:, blk(c)] for c in range(ncb)]
    car = [jnp.zeros((1, LANES), F32)] * ncb
    pp = pp_ref[...]
    p_pw = [pp[:, blk(c)] for c in range(ncb)]
    for s in range(SCAN_ROWS):
        rows = slice(s * seg_rows, (s + 1) * seg_rows)
        nxt = [None] * ncb
        for c in range(hb):
            fr, fi = cstep(p_pw, car, c)
            sp_ref[rows, blk(c)] = (
                e_ref.at[c][pl.ds(s, seg_rows, stride=SCAN_ROWS), :] + fr).astype(sp_ref.dtype)
            sp_ref[rows, blk(c + hb)] = (
                e_ref.at[c + hb][pl.ds(s, seg_rows, stride=SCAN_ROWS), :] + fi).astype(sp_ref.dtype)
            nr, ni = cstep(a_sg, car, c)
            nxt[c] = nr + end[c][s:s + 1, :]
            nxt[c + hb] = ni + end[c + hb][s:s + 1, :]
        car = nxt

    y = jnp.dot(u, t_ref[...], preferred_element_type=F32)
    y = y + jnp.dot(sp_ref[...], wout_ref[...], preferred_element_type=F32)
    y = y + d_ref[...] * u.astype(F32)

    tc = SSM_CHUNK
    ng = y.shape[0] // SCAN_ROWS
    for k in range(tc):
        z_ref[:, k * SCAN_ROWS:(k + 1) * SCAN_ROWS, :] = (
            y[:, blk(k)].reshape(ng, SCAN_ROWS, LANES))
    for c in range(SCAN_ROWS):
        y_ref[:, c * tc:(c + 1) * tc, :] = z_ref[:, pl.ds(c, tc, stride=SCAN_ROWS), :]


def _ssm(u, tables, *, bsz, seg_rows):
    t_mat, w_st, w_out, a_chunk, a_seg, p_pow, d_row = tables
    jb, m, kc = u.shape
    nc = m // bsz
    s2 = w_st.shape[2]
    assert nc == SCAN_ROWS * seg_rows
    kern = functools.partial(_ssm_kernel, seg_rows=seg_rows)
    wspec = lambda r, c: pl.BlockSpec((None, r, c), lambda j, b: (j, 0, 0))
    return pl.pallas_call(
        kern,
        grid=(jb, bsz),
        in_specs=[pl.BlockSpec((None, nc, kc), lambda j, b: (j, b, 0)),
                  wspec(kc, kc), wspec(kc, s2), wspec(s2, kc),
                  wspec(1, s2), wspec(1, s2), wspec(seg_rows, s2), wspec(1, kc)],
        out_specs=pl.BlockSpec((nc // SCAN_ROWS, SCAN_ROWS * SSM_CHUNK, LANES),
                               lambda j, b: (b, 0, j)),
        out_shape=jax.ShapeDtypeStruct((m // SCAN_ROWS, SCAN_ROWS * SSM_CHUNK, jb * LANES), F32),
        scratch_shapes=[pltpu.VMEM((s2 // LANES, SCAN_ROWS * _segment_pitch(seg_rows), LANES), F32),
                        pltpu.VMEM((s2 // LANES, nc, LANES), F32),
                        pltpu.VMEM((nc, s2), BF16),
                        pltpu.VMEM((nc // SCAN_ROWS, SCAN_ROWS * SSM_CHUNK, LANES), F32)],
        compiler_params=_params("parallel", "arbitrary"),
        name="s5_scan",
    )(u, t_mat, w_st, w_out, a_chunk, a_seg, p_pow, d_row)


def _merge_kernel(x_ref, ya_ref, ys_ref, ga_ref, gs_ref, wglu_ref, bglu_ref, wa_ref,
                  ws_ref, wo_ref, npost_ref, npre_ref, h_ref, z_ref):
    ys = jax.nn.gelu(ys_ref[...].astype(F32), approximate=True)
    gate = jnp.dot(ys.astype(BF16), wglu_ref[...], preferred_element_type=F32)
    ys = ys * jax.nn.sigmoid(gate + bglu_ref[...])
    ma = jnp.dot(ya_ref[...], wa_ref[...], preferred_element_type=F32)
    ms = jnp.dot(ys.astype(BF16), ws_ref[...], preferred_element_type=F32)
    merged = (jax.nn.sigmoid(ga_ref[...].astype(F32)) * ma
              + jax.nn.sigmoid(gs_ref[...].astype(F32)) * ms)
    mix = jnp.dot(merged.astype(BF16), wo_ref[...], preferred_element_type=F32)
    h = x_ref[...] + _rms(mix, npost_ref[...])
    h_ref[...] = h
    z_ref[...] = _rms(h, npre_ref[...]).astype(z_ref.dtype)


def _merge(x2, y_a, y_s, proj, w_glu, b_glu, w_a, w_s, w_o, n_post, n_pre, *, ga_blk):
    t, d = x2.shape
    wa = y_a.shape[1]
    wsm = y_s.shape[1]
    tm = _tile(t, 256)
    row = lambda c: pl.BlockSpec((tm, c), lambda i: (i, 0))
    full = lambda a: pl.BlockSpec(a.shape, lambda i: (0, 0), pipeline_mode=pl.Buffered(1))
    return pl.pallas_call(
        _merge_kernel,
        grid=(t // tm,),
        in_specs=[row(d), row(wa), row(wsm),
                  pl.BlockSpec((tm, d), lambda i: (i, ga_blk)),
                  pl.BlockSpec((tm, d), lambda i: (i, ga_blk + 1)),
                  full(w_glu), full(b_glu), full(w_a), full(w_s), full(w_o),
                  full(n_post), full(n_pre)],
        out_specs=[row(d), row(d)],
        out_shape=[jax.ShapeDtypeStruct((t, d), F32),
                   jax.ShapeDtypeStruct((t, d), BF16)],
        compiler_params=_params("parallel"),
        name="merge",
    )(x2, y_a, y_s, proj, proj, w_glu, b_glu, w_a, w_s, w_o, n_post, n_pre)


def _ffn_kernel(z_ref, h_ref, wg_ref, wu_ref, wd_ref, npost_ref, o_ref):
    j = pl.program_id(1)

    @pl.when(j == 0)
    def _():
        o_ref[...] = jnp.zeros(o_ref.shape, o_ref.dtype)

    z = z_ref[...]
    g = jnp.dot(z, wg_ref[...], preferred_element_type=F32)
    u = jnp.dot(z, wu_ref[...], preferred_element_type=F32)
    a = (jax.nn.silu(g) * u).astype(BF16)
    o_ref[...] += jnp.dot(a, wd_ref[...], preferred_element_type=F32)

    @pl.when(j == pl.num_programs(1) - 1)
    def _():
        o_ref[...] = h_ref[...] + _rms(o_ref[...], npost_ref[...])


def _ffn(z, h, w_g, w_u, w_d, n_post):
    t, d = z.shape
    f = w_g.shape[1]
    tm = _tile(t, 512)
    tf = _tile(f, 512)
    return pl.pallas_call(
        _ffn_kernel,
        grid=(t // tm, f // tf),
        in_specs=[pl.BlockSpec((tm, d), lambda i, j: (i, 0)),
                  pl.BlockSpec((tm, d), lambda i, j: (i, 0)),
                  pl.BlockSpec((d, tf), lambda i, j: (0, j)),
                  pl.BlockSpec((d, tf), lambda i, j: (0, j)),
                  pl.BlockSpec((tf, d), lambda i, j: (j, 0)),
                  pl.BlockSpec((1, d), lambda i, j: (0, 0))],
        out_specs=pl.BlockSpec((tm, d), lambda i, j: (i, 0)),
        out_shape=jax.ShapeDtypeStruct((t, d), F32),
        compiler_params=_params("parallel", "arbitrary"),
        name="ffn",
    )(z, h, w_g, w_u, w_d, n_post)


def _lambda_init(layer_idx):
    return 0.8 - 0.6 * math.exp(-0.3 * layer_idx)


def kernel(x, w_in, lambda_q1, lambda_k1, lambda_q2, lambda_k2, subln_w, ssm_a_re, ssm_a_im, ssm_log_dt, ssm_b_re, ssm_b_im, ssm_c_re, ssm_c_im, ssm_d, w_glu, b_glu, w_attn_branch, w_ssm_branch, w_out, norm_mix_pre, norm_mix_post, w_ffn_gate, w_ffn_up, w_ffn_down, norm_ffn_pre, norm_ffn_post):
    bsz, seq, d = x.shape
    depth = w_in.shape[0]
    dh = lambda_q1.shape[-1]
    e = 2 * dh
    attn_w = w_attn_branch.shape[1]
    ssm_w = w_ssm_branch.shape[1]
    n_heads = attn_w // e
    jb = ssm_w // LANES
    tc = SSM_CHUNK
    nc = seq // tc
    seg_rows = nc // SCAN_ROWS
    t = bsz * seq
    row = lambda v: v.reshape(1, -1).astype(F32)

    tm = _tile(seq, 512)
    qscale = dh ** -0.5 * math.log2(math.e)

    h = x.reshape(t, d)
    for l in range(depth):
        lam_init = _lambda_init(l)
        wq, wk, wv, wsi, wga, wgs = jnp.split(
            w_in[l], [attn_w, 2 * attn_w, 3 * attn_w, 3 * attn_w + ssm_w, 3 * attn_w + ssm_w + d],
            axis=1)
        w_main = jnp.concatenate([wq * qscale, wk, wga, wgs], axis=1).astype(BF16)
        proj, vt3, u = _in_proj(h, row(norm_mix_pre[l]), w_main, wv.T.astype(BF16),
                                wsi.astype(BF16), tm=tm, tn=d, e=e)
        y_a = _attention(proj.reshape(bsz, seq, -1), vt3, row(lambda_q1[l]), row(lambda_k1[l]),
                         row(lambda_q2[l]), row(lambda_k2[l]),
                         subln_w[l].reshape(e, 1).astype(F32),
                         n_heads=n_heads, dh=dh, lam_init=lam_init, tq=tm)
        y_a = y_a.reshape(t, attn_w)

        tables = _ssm_tables(ssm_a_re[l], ssm_a_im[l], ssm_log_dt[l], ssm_b_re[l],
                             ssm_b_im[l], ssm_c_re[l], ssm_c_im[l], ssm_d[l],
                             seg_rows=seg_rows)
        y_s = _ssm(u, tables, bsz=bsz, seg_rows=seg_rows)
        y_s = y_s.reshape(t, ssm_w)

        h1, z = _merge(h, y_a, y_s, proj, w_glu[l].astype(BF16), row(b_glu[l]),
                       w_attn_branch[l].astype(BF16), w_ssm_branch[l].astype(BF16),
                       w_out[l].astype(BF16), row(norm_mix_post[l]), row(norm_ffn_pre[l]),
                       ga_blk=2 * attn_w // d)
        h = _ffn(z, h1, w_ffn_gate[l].astype(BF16), w_ffn_up[l].astype(BF16),
                 w_ffn_down[l].astype(BF16), row(norm_ffn_post[l]))
    return h.reshape(bsz, seq, d)
```

```python
import functools
import math

import jax
import jax.numpy as jnp
from jax import lax
from jax.experimental import pallas as pl
from jax.experimental.pallas import tpu as pltpu

EPS = 1e-6
NEG_INF = -1e30
LANES = 128
SSM_CHUNK = 8
SCAN_ROWS = 8
ONES_ROWS = 16
VMEM_LIMIT_BYTES = 56 * 1024 * 1024
F32 = jnp.float32
BF16 = jnp.bfloat16
_NT = (((1,), (1,)), ((), ()))


def _params(*sem):
    return pltpu.CompilerParams(dimension_semantics=sem,
                                vmem_limit_bytes=VMEM_LIMIT_BYTES)


def _rms(x, gain):
    ms = jnp.mean(x * x, axis=-1, keepdims=True)
    return x * lax.rsqrt(ms + EPS) * gain


def _tile(n, pref):
    t = min(n, pref)
    assert n % t == 0, (n, t)
    return t


def _in_proj_kernel(x_ref, g_ref, w_ref, wvt_ref, ws_ref, o_ref, vt_ref, uc_ref,
                    u_ref, s_ref, *, e):
    @pl.when(pl.program_id(1) == 0)
    def _():
        u = _rms(x_ref[...], g_ref[...]).astype(u_ref.dtype)
        u_ref[...] = u
        vt = lax.dot_general(wvt_ref[...], u, _NT, preferred_element_type=F32)
        ea = e + ONES_ROWS
        for hd in range(vt.shape[0] // e):
            vt_ref[hd * ea:hd * ea + e, :] = vt[hd * e:(hd + 1) * e, :].astype(vt_ref.dtype)
            vt_ref[hd * ea + e:(hd + 1) * ea, :] = jnp.ones((ONES_ROWS, vt.shape[1]), vt_ref.dtype)
        s = jnp.dot(u, ws_ref[...], preferred_element_type=F32)
        nrow = uc_ref.shape[1]
        for j in range(s_ref.shape[0]):
            s_ref[j] = s[:, j * LANES:(j + 1) * LANES]
        for j in range(s_ref.shape[0]):
            for k in range(SSM_CHUNK):
                uc_ref[j, :, k * LANES:(k + 1) * LANES] = (
                    s_ref.at[j][pl.ds(k, nrow, stride=SSM_CHUNK), :].astype(uc_ref.dtype))

    o_ref[...] = jnp.dot(u_ref[...], w_ref[...],
                         preferred_element_type=F32).astype(o_ref.dtype)


def _in_proj(x2, gain, w, wvt, w_s, *, tm, tn, e):
    t, d = x2.shape
    n = w.shape[1]
    nva = wvt.shape[0] // e * (e + ONES_ROWS)
    jb = w_s.shape[1] // LANES
    kern = functools.partial(_in_proj_kernel, e=e)
    const = lambda a: pl.BlockSpec(a.shape, lambda i, j: (0, 0), pipeline_mode=pl.Buffered(1))
    return pl.pallas_call(
        kern,
        grid=(t // tm, n // tn),
        in_specs=[pl.BlockSpec((tm, d), lambda i, j: (i, 0)),
                  pl.BlockSpec((1, d), lambda i, j: (0, 0)),
                  pl.BlockSpec((d, tn), lambda i, j: (0, j)),
                  const(wvt), const(w_s)],
        out_specs=[pl.BlockSpec((tm, tn), lambda i, j: (i, j)),
                   pl.BlockSpec((None, nva, tm), lambda i, j: (i, 0, 0)),
                   pl.BlockSpec((jb, tm // SSM_CHUNK, SSM_CHUNK * LANES), lambda i, j: (0, i, 0))],
        out_shape=[jax.ShapeDtypeStruct((t, n), BF16),
                   jax.ShapeDtypeStruct((t // tm, nva, tm), BF16),
                   jax.ShapeDtypeStruct((jb, t // SSM_CHUNK, SSM_CHUNK * LANES), BF16)],
        scratch_shapes=[pltpu.VMEM((tm, d), BF16),
                        pltpu.VMEM((jb, tm, LANES), F32)],
        compiler_params=_params("parallel", "arbitrary"),
        name="in_proj",
    )(x2, gain, w, wvt, w_s)


def _attn_kernel(q_ref, k_ref, vt_ref, lq1_ref, lk1_ref, lq2_ref, lk2_ref, sw_ref,
                 o_ref, qs_ref, qns_ref, m_ref, acc_ref, sa_ref, sb_ref, sc_ref,
                 *, tq, tk, dh, lam_init):
    nq = q_ref.shape[0] // tq
    lax.fori_loop(0, nq, functools.partial(
        _attn_q_tile, q_ref, k_ref, vt_ref, lq1_ref, lk1_ref, lq2_ref, lk2_ref, sw_ref,
        o_ref, qs_ref, qns_ref, m_ref, acc_ref, sa_ref, sb_ref, sc_ref,
        nq=nq, tq=tq, tk=tk, dh=dh, lam_init=lam_init), 0)


def _attn_q_tile(q_ref, k_ref, vt_ref, lq1_ref, lk1_ref, lq2_ref, lk2_ref, sw_ref,
                 o_ref, qs_ref, qns_ref, m_ref, acc_ref, sa_ref, sb_ref, sc_ref,
                 qi, carry, *, nq, tq, tk, dh, lam_init):
    def stack(tile, dst_ref):
        rows = pl.ds(pl.multiple_of(tile * tq, tq), tq)
        qt = q_ref[rows, :].astype(F32).T
        row = lax.broadcasted_iota(jnp.int32, qt.shape, 0)
        zero = jnp.zeros_like(qt)
        dst_ref[:, 0:tq] = jnp.where(row < dh, qt, zero).astype(dst_ref.dtype)
        dst_ref[:, tq:2 * tq] = jnp.where(row >= dh, qt, zero).astype(dst_ref.dtype)

    stack(qi, qs_ref)
    stack(jnp.minimum(qi + 1, nq - 1), qns_ref)
    m_ref[...] = jnp.full(m_ref.shape, NEG_INF, F32)
    acc_ref[...] = jnp.zeros(acc_ref.shape, F32)

    def scores(ki, s_ref, stacked_q_ref=qs_ref):
        k = k_ref[pl.ds(pl.multiple_of(ki * tk, tk), tk), :]
        s_ref[...] = jnp.dot(k, stacked_q_ref[...],
                             preferred_element_type=F32)

    def scores_next():
        scores(0, sc_ref, qns_ref)

    def consume(ki, s_ref, masked):
        s = s_ref[...]
        if masked:
            kpos = ki * tk + lax.broadcasted_iota(jnp.int32, s.shape, 0)
            col = lax.broadcasted_iota(jnp.int32, s.shape, 1)
            qpos = qi * tq + jnp.where(col >= tq, col - tq, col)
            s = jnp.where(kpos <= qpos, s, NEG_INF)
        m_prev = m_ref[...]
        m_new = jnp.maximum(m_prev, jnp.max(s, axis=0, keepdims=True))
        alpha = jnp.exp2(m_prev - m_new)
        p = jnp.exp2(s - m_new)
        acc_ref[...] = alpha * acc_ref[...] + jnp.dot(
            vt_ref[ki], p.astype(vt_ref.dtype), preferred_element_type=F32)
        m_ref[...] = m_new

    @pl.when(qi == 0)
    def _():
        scores(0, sa_ref)
        scores_next()
        consume(0, sa_ref, True)

    @pl.when(qi > 0)
    def _():
        scores(1, sb_ref)
        consume(0, sc_ref, False)

        def pair(t):
            scores(t + 1, sa_ref)
            consume(t, sb_ref, False)
            scores(t + 2, sb_ref)
            consume(t + 1, sa_ref, False)

        def quad(j, carry):
            pair(4 * j + 1)
            pair(4 * j + 3)
            return carry

        nquad = (qi - 1) // 4
        lax.fori_loop(0, nquad, quad, 0)

        @pl.when((qi - 1) % 4 >= 2)
        def _():
            pair(4 * nquad + 1)

        @pl.when(qi % 2 == 1)
        def _():
            scores_next()
            consume(qi, sb_ref, True)

        @pl.when(qi % 2 == 0)
        def _():
            scores(qi, sa_ref)
            consume(qi - 1, sb_ref, False)
            scores_next()
            consume(qi, sa_ref, True)

    lam = (jnp.exp(jnp.sum(lq1_ref[...] * lk1_ref[...], axis=-1, keepdims=True))
           - jnp.exp(jnp.sum(lq2_ref[...] * lk2_ref[...], axis=-1, keepdims=True))
           + lam_init)
    e = 2 * dh
    acc = acc_ref[...]
    o_all = acc[0:e, :] / acc[e:e + 1, :]
    o = o_all[:, 0:tq] - lam * o_all[:, tq:2 * tq]
    ms = jnp.mean(o * o, axis=0, keepdims=True)
    o = o * lax.rsqrt(ms + EPS) * sw_ref[...] * (1.0 - lam_init)
    o_ref[pl.ds(pl.multiple_of(qi * tq, tq), tq), :] = o.T.astype(o_ref.dtype)
    return carry


def _attention(proj3, vt3, lq1, lk1, lq2, lk2, subln, *, n_heads, dh, lam_init, tq):
    bsz, seq, _ = proj3.shape
    e = 2 * dh
    nk = seq // tq
    ea = e + ONES_ROWS
    assert vt3.shape == (bsz * nk, n_heads * ea, tq)
    kern = functools.partial(_attn_kernel, tq=tq, tk=tq, dh=dh, lam_init=lam_init)
    vec = pl.BlockSpec((1, dh), lambda b, h: (0, 0))
    return pl.pallas_call(
        kern,
        grid=(bsz, n_heads),
        in_specs=[pl.BlockSpec((None, seq, e), lambda b, h: (b, 0, h)),
                  pl.BlockSpec((None, seq, e), lambda b, h: (b, 0, n_heads + h)),
                  pl.BlockSpec((nk, ea, tq), lambda b, h: (b, h, 0)),
                  vec, vec, vec, vec,
                  pl.BlockSpec((e, 1), lambda b, h: (0, 0))],
        out_specs=pl.BlockSpec((None, seq, e), lambda b, h: (b, 0, h)),
        out_shape=jax.ShapeDtypeStruct((bsz, seq, n_heads * e), BF16),
        scratch_shapes=[pltpu.VMEM((e, 2 * tq), BF16),
                        pltpu.VMEM((e, 2 * tq), BF16),
                        pltpu.VMEM((1, 2 * tq), F32),
                        pltpu.VMEM((ea, 2 * tq), F32),
                        pltpu.VMEM((tq, 2 * tq), F32),
                        pltpu.VMEM((tq, 2 * tq), F32),
                        pltpu.VMEM((tq, 2 * tq), F32)],
        compiler_params=_params("parallel", "parallel"),
        name="diff_attention",
    )(proj3, proj3, vt3, lq1, lk1, lq2, lk2, subln)


def _ssm_tables(a_re, a_im, log_dt, b_re, b_im, c_re, c_im, d_skip, *, seg_rows):
    g, p, h = b_re.shape
    tc = SSM_CHUNK
    gpb = LANES // h
    jb = g // gpb
    dt = jnp.exp(log_dt)[:, None]
    er, ei = a_re * dt, a_im * dt

    def powers(n):
        n = jnp.asarray(n, F32).reshape((-1, 1, 1))
        mag = jnp.exp(n * er)
        return mag * jnp.cos(n * ei), mag * jnp.sin(n * ei)

    ar, ai = powers([1.0])
    ar, ai = ar[0], ai[0]
    den = a_re * a_re + a_im * a_im
    fr = ((ar - 1.0) * a_re + ai * a_im) / den
    fi = (ai * a_re - (ar - 1.0) * a_im) / den
    bbr = fr[..., None] * b_re - fi[..., None] * b_im
    bbi = fr[..., None] * b_im + fi[..., None] * b_re

    pr, pi = powers(jnp.arange(tc + 1))
    cpr = c_re[None] * pr[:, :, None, :] - c_im[None] * pi[:, :, None, :]
    cpi = c_re[None] * pi[:, :, None, :] + c_im[None] * pr[:, :, None, :]
    kern = (jnp.einsum('tgop,gpi->tgio', cpr[:tc], bbr)
            - jnp.einsum('tgop,gpi->tgio', cpi[:tc], bbi))
    def group_mask(rows_per_group, cols_per_group):
        r = jnp.arange(gpb * rows_per_group) // rows_per_group
        c = jnp.arange(gpb * cols_per_group) // cols_per_group
        return r[:, None] == c[None, :]

    def block_diag(x, rows_per_group, cols_per_group):
        x = jnp.tile(x, (1,) * (x.ndim - 1) + (gpb,))
        return jnp.where(group_mask(rows_per_group, cols_per_group), x, 0.0)

    kidx = jnp.arange(tc)
    tau = kidx[None, :] - kidx[:, None]
    bd = block_diag(kern.reshape(tc, jb, LANES, h), h, h)
    kt = jnp.where((tau >= 0)[:, :, None, None, None], bd[jnp.clip(tau, 0)], 0.0)
    t_mat = jnp.transpose(kt, (2, 0, 3, 1, 4)).reshape(jb, tc * LANES, tc * LANES)

    qr, qi_ = pr[tc - 1 - kidx], pi[tc - 1 - kidx]
    sr = qr[:, :, None, :] * jnp.swapaxes(bbr, 1, 2)[None] - qi_[:, :, None, :] * jnp.swapaxes(bbi, 1, 2)[None]
    si = qr[:, :, None, :] * jnp.swapaxes(bbi, 1, 2)[None] + qi_[:, :, None, :] * jnp.swapaxes(bbr, 1, 2)[None]
    st = jnp.stack([sr, si], 0).reshape(2, tc, jb, LANES, p)
    st = block_diag(st, h, p)
    w_st = jnp.transpose(st, (2, 1, 3, 0, 4)).reshape(jb, tc * LANES, 2 * gpb * p)

    ot = jnp.stack([cpr[1:], -cpi[1:]], 0)
    ot = jnp.swapaxes(ot, 3, 4).reshape(2, tc, jb, gpb * p, h)
    ot = block_diag(ot, p, h)
    w_out = jnp.transpose(ot, (2, 0, 3, 1, 4)).reshape(jb, 2 * gpb * p, tc * LANES)

    def rows(n):
        xr, xi = powers(n)
        x = jnp.stack([xr, xi], 1).reshape(-1, 2, jb, gpb * p)
        return jnp.transpose(x, (2, 0, 1, 3)).reshape(jb, -1, 2 * gpb * p)

    a_chunk = rows([float(tc)])
    a_seg = rows([float(tc * seg_rows)])
    p_pow = rows(tc * jnp.arange(seg_rows))
    d_row = jnp.tile(d_skip.reshape(jb, 1, LANES), (1, 1, tc))
    return (t_mat.astype(BF16), w_st.astype(BF16), w_out.astype(BF16),
            a_chunk, a_seg, p_pow, d_row)


def _segment_pitch(seg_rows):
    return seg_rows + SCAN_ROWS


def _ssm_kernel(u_ref, t_ref, wst_ref, wout_ref, ach_ref, aseg_ref, pp_ref, d_ref,
                y_ref, xst_ref, e_ref, sp_ref, z_ref, *, seg_rows):
    ncb = xst_ref.shape[0]
    hb = ncb // 2
    pitch = _segment_pitch(seg_rows)
    blk = lambda c: slice(c * LANES, (c + 1) * LANES)
    u = u_ref[...]
    xst = jnp.dot(u, wst_ref[...], preferred_element_type=F32)
    for c in range(ncb):
        for s in range(SCAN_ROWS):
            xst_ref[c, s * pitch:s * pitch + seg_rows, :] = (
                xst[s * seg_rows:(s + 1) * seg_rows, blk(c)])

    ach = ach_ref[...]
    a_ch = [jnp.broadcast_to(ach[:, blk(c)], (SCAN_ROWS, LANES)) for c in range(ncb)]

    def cstep(a, e, c):
        return (a[c] * e[c] - a[c + hb] * e[c + hb], a[c] * e[c + hb] + a[c + hb] * e[c])

    def scan_body(i, e):
        row0 = pl.multiple_of(i * SCAN_ROWS, SCAN_ROWS)
        new = [None] * ncb
        for c in range(hb):
            e_ref[c, pl.ds(row0, SCAN_ROWS), :] = e[c]
            e_ref[c + hb, pl.ds(row0, SCAN_ROWS), :] = e[c + hb]
            nr, ni = cstep(a_ch, e, c)
            new[c] = nr + xst_ref.at[c][pl.ds(i, SCAN_ROWS, stride=pitch), :]
            new[c + hb] = ni + xst_ref.at[c + hb][pl.ds(i, SCAN_ROWS, stride=pitch), :]
        return tuple(new)

    zero = jnp.zeros((SCAN_ROWS, LANES), F32)
    end = lax.fori_loop(0, seg_rows, scan_body, (zero,) * ncb, unroll=4)

    asg = aseg_ref[...]
    a_sg = [asg[:, blk(c)] for c in range(ncb)]
    car = [jnp.zeros((1, LANES), F32)] * ncb
    pp = pp_ref[...]
    p_pw = [pp[:, blk(c)] for c in range(ncb)]
    for s in range(SCAN_ROWS):
        rows = slice(s * seg_rows, (s + 1) * seg_rows)
        nxt = [None] * ncb
        for c in range(hb):
            fr, fi = cstep(p_pw, car, c)
            sp_ref[rows, blk(c)] = (
                e_ref.at[c][pl.ds(s, seg_rows, stride=SCAN_ROWS), :] + fr).astype(sp_ref.dtype)
            sp_ref[rows, blk(c + hb)] = (
                e_ref.at[c + hb][pl.ds(s, seg_rows, stride=SCAN_ROWS), :] + fi).astype(sp_ref.dtype)
            nr, ni = cstep(a_sg, car, c)
            nxt[c] = nr + end[c][s:s + 1, :]
            nxt[c + hb] = ni + end[c + hb][s:s + 1, :]
        car = nxt

    y = jnp.dot(u, t_ref[...], preferred_element_type=F32)
    y = y + jnp.dot(sp_ref[...], wout_ref[...], preferred_element_type=F32)
    y = y + d_ref[...] * u.astype(F32)

    tc = SSM_CHUNK
    ng = y.shape[0] // SCAN_ROWS
    for k in range(tc):
        z_ref[:, k * SCAN_ROWS:(k + 1) * SCAN_ROWS, :] = (
            y[:, blk(k)].reshape(ng, SCAN_ROWS, LANES))
    for c in range(SCAN_ROWS):
        y_ref[:, c * tc:(c + 1) * tc, :] = z_ref[:, pl.ds(c, tc, stride=SCAN_ROWS), :]


def _ssm(u, tables, *, bsz, seg_rows):
    t_mat, w_st, w_out, a_chunk, a_seg, p_pow, d_row = tables
    jb, m, kc = u.shape
    nc = m // bsz
    s2 = w_st.shape[2]
    assert nc == SCAN_ROWS * seg_rows
    kern = functools.partial(_ssm_kernel, seg_rows=seg_rows)
    wspec = lambda r, c: pl.BlockSpec((None, r, c), lambda j, b: (j, 0, 0))
    return pl.pallas_call(
        kern,
        grid=(jb, bsz),
        in_specs=[pl.BlockSpec((None, nc, kc), lambda j, b: (j, b, 0)),
                  wspec(kc, kc), wspec(kc, s2), wspec(s2, kc),
                  wspec(1, s2), wspec(1, s2), wspec(seg_rows, s2), wspec(1, kc)],
        out_specs=pl.BlockSpec((nc // SCAN_ROWS, SCAN_ROWS * SSM_CHUNK, LANES),
                               lambda j, b: (b, 0, j)),
        out_shape=jax.ShapeDtypeStruct((m // SCAN_ROWS, SCAN_ROWS * SSM_CHUNK, jb * LANES), F32),
        scratch_shapes=[pltpu.VMEM((s2 // LANES, SCAN_ROWS * _segment_pitch(seg_rows), LANES), F32),
                        pltpu.VMEM((s2 // LANES, nc, LANES), F32),
                        pltpu.VMEM((nc, s2), BF16),
                        pltpu.VMEM((nc // SCAN_ROWS, SCAN_ROWS * SSM_CHUNK, LANES), F32)],
        compiler_params=_params("parallel", "arbitrary"),
        name="s5_scan",
    )(u, t_mat, w_st, w_out, a_chunk, a_seg, p_pow, d_row)


def _merge_kernel(x_ref, ya_ref, ys_ref, ga_ref, gs_ref, wglu_ref, bglu_ref, wa_ref,
                  ws_ref, wo_ref, npost_ref, npre_ref, h_ref, z_ref):
    ys = jax.nn.gelu(ys_ref[...].astype(F32), approximate=True)
    gate = jnp.dot(ys.astype(BF16), wglu_ref[...], preferred_element_type=F32)
    ys = ys * jax.nn.sigmoid(gate + bglu_ref[...])
    ma = jnp.dot(ya_ref[...], wa_ref[...], preferred_element_type=F32)
    ms = jnp.dot(ys.astype(BF16), ws_ref[...], preferred_element_type=F32)
    merged = (jax.nn.sigmoid(ga_ref[...].astype(F32)) * ma
              + jax.nn.sigmoid(gs_ref[...].astype(F32)) * ms)
    mix = jnp.dot(merged.astype(BF16), wo_ref[...], preferred_element_type=F32)
    h = x_ref[...] + _rms(mix, npost_ref[...])
    h_ref[...] = h
    z_ref[...] = _rms(h, npre_ref[...]).astype(z_ref.dtype)


def _merge(x2, y_a, y_s, proj, w_glu, b_glu, w_a, w_s, w_o, n_post, n_pre, *, ga_blk):
    t, d = x2.shape
    wa = y_a.shape[1]
    wsm = y_s.shape[1]
    tm = _tile(t, 256)
    row = lambda c: pl.BlockSpec((tm, c), lambda i: (i, 0))
    full = lambda a: pl.BlockSpec(a.shape, lambda i: (0, 0), pipeline_mode=pl.Buffered(1))
    return pl.pallas_call(
        _merge_kernel,
        grid=(t // tm,),
        in_specs=[row(d), row(wa), row(wsm),
                  pl.BlockSpec((tm, d), lambda i: (i, ga_blk)),
                  pl.BlockSpec((tm, d), lambda i: (i, ga_blk + 1)),
                  full(w_glu), full(b_glu), full(w_a), full(w_s), full(w_o),
                  full(n_post), full(n_pre)],
        out_specs=[row(d), row(d)],
        out_shape=[jax.ShapeDtypeStruct((t, d), F32),
                   jax.ShapeDtypeStruct((t, d), BF16)],
        compiler_params=_params("parallel"),
        name="merge",
    )(x2, y_a, y_s, proj, proj, w_glu, b_glu, w_a, w_s, w_o, n_post, n_pre)


def _ffn_kernel(z_ref, h_ref, wg_hbm, wu_hbm, wd_hbm, npost_ref, o_ref,
                wg_buf, wu_buf, wd_buf, sem, *, nf, tf):
    i = pl.program_id(0)
    last_row = i == pl.num_programs(0) - 1

    def copies(j, slot):
        cols = pl.ds(pl.multiple_of(j * tf, tf), tf)
        return (pltpu.make_async_copy(wg_hbm.at[:, cols], wg_buf.at[slot], sem.at[0, slot]),
                pltpu.make_async_copy(wu_hbm.at[:, cols], wu_buf.at[slot], sem.at[1, slot]),
                pltpu.make_async_copy(wd_hbm.at[cols, :], wd_buf.at[slot], sem.at[2, slot]))

    @pl.when(i == 0)
    def _():
        for c in copies(0, 0):
            c.start()

    o_ref[...] = jnp.zeros(o_ref.shape, o_ref.dtype)
    z = z_ref[...]

    def body(j, carry):
        slot = (i * nf + j) % 2
        for c in copies(j, slot):
            c.wait()

        @pl.when(jnp.logical_not(jnp.logical_and(last_row, j == nf - 1)))
        def _():
            for c in copies((j + 1) % nf, 1 - slot):
                c.start()

        g = jnp.dot(z, wg_buf[slot], preferred_element_type=F32)
        u = jnp.dot(z, wu_buf[slot], preferred_element_type=F32)
        a = (jax.nn.silu(g) * u).astype(BF16)
        o_ref[...] += jnp.dot(a, wd_buf[slot], preferred_element_type=F32)
        return carry

    lax.fori_loop(0, nf, body, 0)
    o_ref[...] = h_ref[...] + _rms(o_ref[...], npost_ref[...])


def _ffn(z, h, w_g, w_u, w_d, n_post):
    t, d = z.shape
    f = w_g.shape[1]
    tm = _tile(t, 512)
    tf = _tile(f, 512)
    kern = functools.partial(_ffn_kernel, nf=f // tf, tf=tf)
    hbm = pl.BlockSpec(memory_space=pl.ANY)
    return pl.pallas_call(
        kern,
        grid=(t // tm,),
        in_specs=[pl.BlockSpec((tm, d), lambda i: (i, 0)),
                  pl.BlockSpec((tm, d), lambda i: (i, 0)),
                  hbm, hbm, hbm,
                  pl.BlockSpec((1, d), lambda i: (0, 0))],
        out_specs=pl.BlockSpec((tm, d), lambda i: (i, 0)),
        out_shape=jax.ShapeDtypeStruct((t, d), F32),
        scratch_shapes=[pltpu.VMEM((2, d, tf), BF16),
                        pltpu.VMEM((2, d, tf), BF16),
                        pltpu.VMEM((2, tf, d), BF16),
                        pltpu.SemaphoreType.DMA((3, 2))],
        compiler_params=_params("arbitrary"),
        name="ffn",
    )(z, h, w_g, w_u, w_d, n_post)


def _lambda_init(layer_idx):
    return 0.8 - 0.6 * math.exp(-0.3 * layer_idx)


def kernel(x, w_in, lambda_q1, lambda_k1, lambda_q2, lambda_k2, subln_w, ssm_a_re, ssm_a_im, ssm_log_dt, ssm_b_re, ssm_b_im, ssm_c_re, ssm_c_im, ssm_d, w_glu, b_glu, w_attn_branch, w_ssm_branch, w_out, norm_mix_pre, norm_mix_post, w_ffn_gate, w_ffn_up, w_ffn_down, norm_ffn_pre, norm_ffn_post):
    bsz, seq, d = x.shape
    depth = w_in.shape[0]
    dh = lambda_q1.shape[-1]
    e = 2 * dh
    attn_w = w_attn_branch.shape[1]
    ssm_w = w_ssm_branch.shape[1]
    n_heads = attn_w // e
    jb = ssm_w // LANES
    tc = SSM_CHUNK
    nc = seq // tc
    seg_rows = nc // SCAN_ROWS
    t = bsz * seq
    row = lambda v: v.reshape(1, -1).astype(F32)

    tm = _tile(seq, 512)
    qscale = dh ** -0.5 * math.log2(math.e)

    h = x.reshape(t, d)
    for l in range(depth):
        lam_init = _lambda_init(l)
        wq, wk, wv, wsi, wga, wgs = jnp.split(
            w_in[l], [attn_w, 2 * attn_w, 3 * attn_w, 3 * attn_w + ssm_w, 3 * attn_w + ssm_w + d],
            axis=1)
        w_main = jnp.concatenate([wq * qscale, wk, wga, wgs], axis=1).astype(BF16)
        proj, vt3, u = _in_proj(h, row(norm_mix_pre[l]), w_main, wv.T.astype(BF16),
                                wsi.astype(BF16), tm=tm, tn=d, e=e)
        y_a = _attention(proj.reshape(bsz, seq, -1), vt3, row(lambda_q1[l]), row(lambda_k1[l]),
                         row(lambda_q2[l]), row(lambda_k2[l]),
                         subln_w[l].reshape(e, 1).astype(F32),
                         n_heads=n_heads, dh=dh, lam_init=lam_init, tq=tm)
        y_a = y_a.reshape(t, attn_w)

        tables = _ssm_tables(ssm_a_re[l], ssm_a_im[l], ssm_log_dt[l], ssm_b_re[l],
                             ssm_b_im[l], ssm_c_re[l], ssm_c_im[l], ssm_d[l],
                             seg_rows=seg_rows)
        y_s = _ssm(u, tables, bsz=bsz, seg_rows=seg_rows)
        y_s = y_s.reshape(t, ssm_w)

        h1, z = _merge(h, y_a, y_s, proj, w_glu[l].astype(BF16), row(b_glu[l]),
                       w_attn_branch[l].astype(BF16), w_ssm_branch[l].astype(BF16),
                       w_out[l].astype(BF16), row(norm_mix_post[l]), row(norm_ffn_pre[l]),
                       ga_blk=2 * attn_w // d)
        h = _ffn(z, h1, w_ffn_gate[l].astype(BF16), w_ffn_up[l].astype(BF16),
                 w_ffn_down[l].astype(BF16), row(norm_ffn_post[l]))
    return h.reshape(bsz, seq, d)
```

```python
import functools
import math

import jax
import jax.numpy as jnp
from jax import lax
from jax.experimental import pallas as pl
from jax.experimental.pallas import tpu as pltpu

EPS = 1e-6
NEG_INF = -1e30
LANES = 128
SSM_CHUNK = 8
SCAN_ROWS = 8
ONES_ROWS = 16
VMEM_LIMIT_BYTES = 56 * 1024 * 1024
F32 = jnp.float32
BF16 = jnp.bfloat16
_NT = (((1,), (1,)), ((), ()))


def _params(*sem):
    return pltpu.CompilerParams(dimension_semantics=sem,
                                vmem_limit_bytes=VMEM_LIMIT_BYTES)


def _rms(x, gain):
    ms = jnp.mean(x * x, axis=-1, keepdims=True)
    return x * lax.rsqrt(ms + EPS) * gain


def _tile(n, pref):
    t = min(n, pref)
    assert n % t == 0, (n, t)
    return t


def _in_proj_kernel(x_ref, g_ref, w_ref, wvt_ref, ws_ref, o_ref, vt_ref, uc_ref,
                    u_ref, s_ref, *, e):
    @pl.when(pl.program_id(1) == 0)
    def _():
        u = _rms(x_ref[...], g_ref[...]).astype(u_ref.dtype)
        u_ref[...] = u
        vt = lax.dot_general(wvt_ref[...], u, _NT, preferred_element_type=F32)
        ea = e + ONES_ROWS
        for hd in range(vt.shape[0] // e):
            vt_ref[hd * ea:hd * ea + e, :] = vt[hd * e:(hd + 1) * e, :].astype(vt_ref.dtype)
            vt_ref[hd * ea + e:(hd + 1) * ea, :] = jnp.ones((ONES_ROWS, vt.shape[1]), vt_ref.dtype)
        s = jnp.dot(u, ws_ref[...], preferred_element_type=F32)
        nrow = uc_ref.shape[1]
        for j in range(s_ref.shape[0]):
            s_ref[j] = s[:, j * LANES:(j + 1) * LANES]
        for j in range(s_ref.shape[0]):
            for k in range(SSM_CHUNK):
                uc_ref[j, :, k * LANES:(k + 1) * LANES] = (
                    s_ref.at[j][pl.ds(k, nrow, stride=SSM_CHUNK), :].astype(uc_ref.dtype))

    o_ref[...] = jnp.dot(u_ref[...], w_ref[...],
                         preferred_element_type=F32).astype(o_ref.dtype)


def _in_proj(x2, gain, w, wvt, w_s, *, tm, tn, e):
    t, d = x2.shape
    n = w.shape[1]
    nva = wvt.shape[0] // e * (e + ONES_ROWS)
    jb = w_s.shape[1] // LANES
    kern = functools.partial(_in_proj_kernel, e=e)
    const = lambda a: pl.BlockSpec(a.shape, lambda i, j: (0, 0), pipeline_mode=pl.Buffered(1))
    return pl.pallas_call(
        kern,
        grid=(t // tm, n // tn),
        in_specs=[pl.BlockSpec((tm, d), lambda i, j: (i, 0)),
                  pl.BlockSpec((1, d), lambda i, j: (0, 0)),
                  pl.BlockSpec((d, tn), lambda i, j: (0, j)),
                  const(wvt), const(w_s)],
        out_specs=[pl.BlockSpec((tm, tn), lambda i, j: (i, j)),
                   pl.BlockSpec((None, nva, tm), lambda i, j: (i, 0, 0)),
                   pl.BlockSpec((jb, tm // SSM_CHUNK, SSM_CHUNK * LANES), lambda i, j: (0, i, 0))],
        out_shape=[jax.ShapeDtypeStruct((t, n), BF16),
                   jax.ShapeDtypeStruct((t // tm, nva, tm), BF16),
                   jax.ShapeDtypeStruct((jb, t // SSM_CHUNK, SSM_CHUNK * LANES), BF16)],
        scratch_shapes=[pltpu.VMEM((tm, d), BF16),
                        pltpu.VMEM((jb, tm, LANES), F32)],
        compiler_params=_params("parallel", "arbitrary"),
        name="in_proj",
    )(x2, gain, w, wvt, w_s)


def _attn_kernel(q_ref, k_ref, vt_ref, lq1_ref, lk1_ref, lq2_ref, lk2_ref, sw_ref,
                 o_ref, qs_ref, qns_ref, m_ref, acc_ref, sa_ref, sb_ref, sc_ref,
                 *, tq, tk, dh, lam_init):
    nq = q_ref.shape[0] // tq
    lax.fori_loop(0, nq, functools.partial(
        _attn_q_tile, q_ref, k_ref, vt_ref, lq1_ref, lk1_ref, lq2_ref, lk2_ref, sw_ref,
        o_ref, qs_ref, qns_ref, m_ref, acc_ref, sa_ref, sb_ref, sc_ref,
        nq=nq, tq=tq, tk=tk, dh=dh, lam_init=lam_init), 0)


def _attn_q_tile(q_ref, k_ref, vt_ref, lq1_ref, lk1_ref, lq2_ref, lk2_ref, sw_ref,
                 o_ref, qs_ref, qns_ref, m_ref, acc_ref, sa_ref, sb_ref, sc_ref,
                 qi, carry, *, nq, tq, tk, dh, lam_init):
    def stack(tile, dst_ref):
        rows = pl.ds(pl.multiple_of(tile * tq, tq), tq)
        qt = q_ref[rows, :].astype(F32).T
        row = lax.broadcasted_iota(jnp.int32, qt.shape, 0)
        zero = jnp.zeros_like(qt)
        dst_ref[:, 0:tq] = jnp.where(row < dh, qt, zero).astype(dst_ref.dtype)
        dst_ref[:, tq:2 * tq] = jnp.where(row >= dh, qt, zero).astype(dst_ref.dtype)

    stack(qi, qs_ref)
    stack(jnp.minimum(qi + 1, nq - 1), qns_ref)
    m_ref[...] = jnp.full(m_ref.shape, NEG_INF, F32)
    acc_ref[...] = jnp.zeros(acc_ref.shape, F32)

    def scores(ki, s_ref, stacked_q_ref=qs_ref):
        k = k_ref[pl.ds(pl.multiple_of(ki * tk, tk), tk), :]
        s_ref[...] = jnp.dot(k, stacked_q_ref[...],
                             preferred_element_type=F32)

    def scores_next():
        scores(0, sc_ref, qns_ref)

    def consume(ki, s_ref, masked):
        s = s_ref[...]
        if masked:
            kpos = ki * tk + lax.broadcasted_iota(jnp.int32, s.shape, 0)
            col = lax.broadcasted_iota(jnp.int32, s.shape, 1)
            qpos = qi * tq + jnp.where(col >= tq, col - tq, col)
            s = jnp.where(kpos <= qpos, s, NEG_INF)
        m_prev = m_ref[...]
        m_new = jnp.maximum(m_prev, jnp.max(s, axis=0, keepdims=True))
        alpha = jnp.exp2(m_prev - m_new)
        p = jnp.exp2(s - m_new)
        acc_ref[...] = alpha * acc_ref[...] + jnp.dot(
            vt_ref[ki], p.astype(vt_ref.dtype), preferred_element_type=F32)
        m_ref[...] = m_new

    @pl.when(qi == 0)
    def _():
        scores(0, sa_ref)
        scores_next()
        consume(0, sa_ref, True)

    @pl.when(qi > 0)
    def _():
        scores(1, sb_ref)
        consume(0, sc_ref, False)

        def pair(t):
            scores(t + 1, sa_ref)
            consume(t, sb_ref, False)
            scores(t + 2, sb_ref)
            consume(t + 1, sa_ref, False)

        def quad(j, carry):
            pair(4 * j + 1)
            pair(4 * j + 3)
            return carry

        nquad = (qi - 1) // 4
        lax.fori_loop(0, nquad, quad, 0)

        @pl.when((qi - 1) % 4 >= 2)
        def _():
            pair(4 * nquad + 1)

        @pl.when(qi % 2 == 1)
        def _():
            scores_next()
            consume(qi, sb_ref, True)

        @pl.when(qi % 2 == 0)
        def _():
            scores(qi, sa_ref)
            consume(qi - 1, sb_ref, False)
            scores_next()
            consume(qi, sa_ref, True)

    lam = (jnp.exp(jnp.sum(lq1_ref[...] * lk1_ref[...], axis=-1, keepdims=True))
           - jnp.exp(jnp.sum(lq2_ref[...] * lk2_ref[...], axis=-1, keepdims=True))
           + lam_init)
    e = 2 * dh
    acc = acc_ref[...]
    o_all = acc[0:e, :] / acc[e:e + 1, :]
    o = o_all[:, 0:tq] - lam * o_all[:, tq:2 * tq]
    ms = jnp.mean(o * o, axis=0, keepdims=True)
    o = o * lax.rsqrt(ms + EPS) * sw_ref[...] * (1.0 - lam_init)
    o_ref[pl.ds(pl.multiple_of(qi * tq, tq), tq), :] = o.T.astype(o_ref.dtype)
    return carry


def _attention(proj3, vt3, lq1, lk1, lq2, lk2, subln, *, n_heads, dh, lam_init, tq):
    bsz, seq, _ = proj3.shape
    e = 2 * dh
    nk = seq // tq
    ea = e + ONES_ROWS
    assert vt3.shape == (bsz * nk, n_heads * ea, tq)
    kern = functools.partial(_attn_kernel, tq=tq, tk=tq, dh=dh, lam_init=lam_init)
    vec = pl.BlockSpec((1, dh), lambda b, h: (0, 0))
    return pl.pallas_call(
        kern,
        grid=(bsz, n_heads),
        in_specs=[pl.BlockSpec((None, seq, e), lambda b, h: (b, 0, h)),
                  pl.BlockSpec((None, seq, e), lambda b, h: (b, 0, n_heads + h)),
                  pl.BlockSpec((nk, ea, tq), lambda b, h: (b, h, 0)),
                  vec, vec, vec, vec,
                  pl.BlockSpec((e, 1), lambda b, h: (0, 0))],
        out_specs=pl.BlockSpec((None, seq, e), lambda b, h: (b, 0, h)),
        out_shape=jax.ShapeDtypeStruct((bsz, seq, n_heads * e), BF16),
        scratch_shapes=[pltpu.VMEM((e, 2 * tq), BF16),
                        pltpu.VMEM((e, 2 * tq), BF16),
                        pltpu.VMEM((1, 2 * tq), F32),
                        pltpu.VMEM((ea, 2 * tq), F32),
                        pltpu.VMEM((tq, 2 * tq), F32),
                        pltpu.VMEM((tq, 2 * tq), F32),
                        pltpu.VMEM((tq, 2 * tq), F32)],
        compiler_params=_params("parallel", "parallel"),
        name="diff_attention",
    )(proj3, proj3, vt3, lq1, lk1, lq2, lk2, subln)


def _ssm_tables(a_re, a_im, log_dt, b_re, b_im, c_re, c_im, d_skip, *, seg_rows):
    g, p, h = b_re.shape
    tc = SSM_CHUNK
    gpb = LANES // h
    jb = g // gpb
    dt = jnp.exp(log_dt)[:, None]
    er, ei = a_re * dt, a_im * dt

    def powers(n):
        n = jnp.asarray(n, F32).reshape((-1, 1, 1))
        mag = jnp.exp(n * er)
        return mag * jnp.cos(n * ei), mag * jnp.sin(n * ei)

    ar, ai = powers([1.0])
    ar, ai = ar[0], ai[0]
    den = a_re * a_re + a_im * a_im
    fr = ((ar - 1.0) * a_re + ai * a_im) / den
    fi = (ai * a_re - (ar - 1.0) * a_im) / den
    bbr = fr[..., None] * b_re - fi[..., None] * b_im
    bbi = fr[..., None] * b_im + fi[..., None] * b_re

    pr, pi = powers(jnp.arange(tc + 1))
    cpr = c_re[None] * pr[:, :, None, :] - c_im[None] * pi[:, :, None, :]
    cpi = c_re[None] * pi[:, :, None, :] + c_im[None] * pr[:, :, None, :]
    kern = (jnp.einsum('tgop,gpi->tgio', cpr[:tc], bbr)
            - jnp.einsum('tgop,gpi->tgio', cpi[:tc], bbi))
    def group_mask(rows_per_group, cols_per_group):
        r = jnp.arange(gpb * rows_per_group) // rows_per_group
        c = jnp.arange(gpb * cols_per_group) // cols_per_group
        return r[:, None] == c[None, :]

    def block_diag(x, rows_per_group, cols_per_group):
        x = jnp.tile(x, (1,) * (x.ndim - 1) + (gpb,))
        return jnp.where(group_mask(rows_per_group, cols_per_group), x, 0.0)

    kidx = jnp.arange(tc)
    tau = kidx[None, :] - kidx[:, None]
    bd = block_diag(kern.reshape(tc, jb, LANES, h), h, h)
    kt = jnp.where((tau >= 0)[:, :, None, None, None], bd[jnp.clip(tau, 0)], 0.0)
    t_mat = jnp.transpose(kt, (2, 0, 3, 1, 4)).reshape(jb, tc * LANES, tc * LANES)

    qr, qi_ = pr[tc - 1 - kidx], pi[tc - 1 - kidx]
    sr = qr[:, :, None, :] * jnp.swapaxes(bbr, 1, 2)[None] - qi_[:, :, None, :] * jnp.swapaxes(bbi, 1, 2)[None]
    si = qr[:, :, None, :] * jnp.swapaxes(bbi, 1, 2)[None] + qi_[:, :, None, :] * jnp.swapaxes(bbr, 1, 2)[None]
    st = jnp.stack([sr, si], 0).reshape(2, tc, jb, LANES, p)
    st = block_diag(st, h, p)
    w_st = jnp.transpose(st, (2, 1, 3, 0, 4)).reshape(jb, tc * LANES, 2 * gpb * p)

    ot = jnp.stack([cpr[1:], -cpi[1:]], 0)
    ot = jnp.swapaxes(ot, 3, 4).reshape(2, tc, jb, gpb * p, h)
    ot = block_diag(ot, p, h)
    w_out = jnp.transpose(ot, (2, 0, 3, 1, 4)).reshape(jb, 2 * gpb * p, tc * LANES)

    def rows(n):
        xr, xi = powers(n)
        x = jnp.stack([xr, xi], 1).reshape(-1, 2, jb, gpb * p)
        return jnp.transpose(x, (2, 0, 1, 3)).reshape(jb, -1, 2 * gpb * p)

    a_chunk = rows([float(tc)])
    a_seg = rows([float(tc * seg_rows)])
    p_pow = rows(tc * jnp.arange(seg_rows))
    d_row = jnp.tile(d_skip.reshape(jb, 1, LANES), (1, 1, tc))
    return (t_mat.astype(BF16), w_st.astype(BF16), w_out.astype(BF16),
            a_chunk, a_seg, p_pow, d_row)


def _segment_pitch(seg_rows):
    return seg_rows + SCAN_ROWS


def _ssm_kernel(u_ref, t_ref, wst_ref, wout_ref, ach_ref, aseg_ref, pp_ref, d_ref,
                y_ref, xst_ref, e_ref, sp_ref, z_ref, *, seg_rows):
    ncb = xst_ref.shape[0]
    hb = ncb // 2
    pitch = _segment_pitch(seg_rows)
    blk = lambda c: slice(c * LANES, (c + 1) * LANES)
    u = u_ref[...]
    xst = jnp.dot(u, wst_ref[...], preferred_element_type=F32)
    for c in range(ncb):
        for s in range(SCAN_ROWS):
            xst_ref[c, s * pitch:s * pitch + seg_rows, :] = (
                xst[s * seg_rows:(s + 1) * seg_rows, blk(c)])

    ach = ach_ref[...]
    a_ch = [jnp.broadcast_to(ach[:, blk(c)], (SCAN_ROWS, LANES)) for c in range(ncb)]

    def cstep(a, e, c):
        return (a[c] * e[c] - a[c + hb] * e[c + hb], a[c] * e[c + hb] + a[c + hb] * e[c])

    def scan_body(i, e):
        row0 = pl.multiple_of(i * SCAN_ROWS, SCAN_ROWS)
        new = [None] * ncb
        for c in range(hb):
            e_ref[c, pl.ds(row0, SCAN_ROWS), :] = e[c]
            e_ref[c + hb, pl.ds(row0, SCAN_ROWS), :] = e[c + hb]
            nr, ni = cstep(a_ch, e, c)
            new[c] = nr + xst_ref.at[c][pl.ds(i, SCAN_ROWS, stride=pitch), :]
            new[c + hb] = ni + xst_ref.at[c + hb][pl.ds(i, SCAN_ROWS, stride=pitch), :]
        return tuple(new)

    zero = jnp.zeros((SCAN_ROWS, LANES), F32)
    end = lax.fori_loop(0, seg_rows, scan_body, (zero,) * ncb, unroll=4)

    asg = aseg_ref[...]
    a_sg = [asg[:, blk(c)] for c in range(ncb)]
    car = [jnp.zeros((1, LANES), F32)] * ncb
    pp = pp_ref[...]
    p_pw = [pp[:, blk(c)] for c in range(ncb)]
    for s in range(SCAN_ROWS):
        rows = slice(s * seg_rows, (s + 1) * seg_rows)
        nxt = [None] * ncb
        for c in range(hb):
            fr, fi = cstep(p_pw, car, c)
            sp_ref[rows, blk(c)] = (
                e_ref.at[c][pl.ds(s, seg_rows, stride=SCAN_ROWS), :] + fr).astype(sp_ref.dtype)
            sp_ref[rows, blk(c + hb)] = (
                e_ref.at[c + hb][pl.ds(s, seg_rows, stride=SCAN_ROWS), :] + fi).astype(sp_ref.dtype)
            nr, ni = cstep(a_sg, car, c)
            nxt[c] = nr + end[c][s:s + 1, :]
            nxt[c + hb] = ni + end[c + hb][s:s + 1, :]
        car = nxt

    y = jnp.dot(u, t_ref[...], preferred_element_type=F32)
    y = y + jnp.dot(sp_ref[...], wout_ref[...], preferred_element_type=F32)
    y = y + d_ref[...] * u.astype(F32)

    tc = SSM_CHUNK
    ng = y.shape[0] // SCAN_ROWS
    for k in range(tc):
        z_ref[:, k * SCAN_ROWS:(k + 1) * SCAN_ROWS, :] = (
            y[:, blk(k)].reshape(ng, SCAN_ROWS, LANES))
    for c in range(SCAN_ROWS):
        y_ref[:, c * tc:(c + 1) * tc, :] = z_ref[:, pl.ds(c, tc, stride=SCAN_ROWS), :]


def _ssm(u, tables, *, bsz, seg_rows):
    t_mat, w_st, w_out, a_chunk, a_seg, p_pow, d_row = tables
    jb, m, kc = u.shape
    nc = m // bsz
    s2 = w_st.shape[2]
    assert nc == SCAN_ROWS * seg_rows
    kern = functools.partial(_ssm_kernel, seg_rows=seg_rows)
    wspec = lambda r, c: pl.BlockSpec((None, r, c), lambda j, b: (j, 0, 0))
    return pl.pallas_call(
        kern,
        grid=(jb, bsz),
        in_specs=[pl.BlockSpec((None, nc, kc), lambda j, b: (j, b, 0)),
                  wspec(kc, kc), wspec(kc, s2), wspec(s2, kc),
                  wspec(1, s2), wspec(1, s2), wspec(seg_rows, s2), wspec(1, kc)],
        out_specs=pl.BlockSpec((nc // SCAN_ROWS, SCAN_ROWS * SSM_CHUNK, LANES),
                               lambda j, b: (b, 0, j)),
        out_shape=jax.ShapeDtypeStruct((m // SCAN_ROWS, SCAN_ROWS * SSM_CHUNK, jb * LANES), F32),
        scratch_shapes=[pltpu.VMEM((s2 // LANES, SCAN_ROWS * _segment_pitch(seg_rows), LANES), F32),
                        pltpu.VMEM((s2 // LANES, nc, LANES), F32),
                        pltpu.VMEM((nc, s2), BF16),
                        pltpu.VMEM((nc // SCAN_ROWS, SCAN_ROWS * SSM_CHUNK, LANES), F32)],
        compiler_params=_params("parallel", "arbitrary"),
        name="s5_scan",
    )(u, t_mat, w_st, w_out, a_chunk, a_seg, p_pow, d_row)


def _merge_kernel(x_ref, ya_ref, ys_ref, ga_ref, gs_ref, wglu_ref, bglu_ref, wa_ref,
                  ws_ref, wo_ref, npost_ref, npre_ref, h_ref, z_ref):
    ys = jax.nn.gelu(ys_ref[...].astype(F32), approximate=True)
    gate = jnp.dot(ys.astype(BF16), wglu_ref[...], preferred_element_type=F32)
    ys = ys * jax.nn.sigmoid(gate + bglu_ref[...])
    ma = jnp.dot(ya_ref[...], wa_ref[...], preferred_element_type=F32)
    ms = jnp.dot(ys.astype(BF16), ws_ref[...], preferred_element_type=F32)
    merged = (jax.nn.sigmoid(ga_ref[...].astype(F32)) * ma
              + jax.nn.sigmoid(gs_ref[...].astype(F32)) * ms)
    mix = jnp.dot(merged.astype(BF16), wo_ref[...], preferred_element_type=F32)
    h = x_ref[...] + _rms(mix, npost_ref[...])
    h_ref[...] = h
    z_ref[...] = _rms(h, npre_ref[...]).astype(z_ref.dtype)


def _merge(x2, y_a, y_s, proj, w_glu, b_glu, w_a, w_s, w_o, n_post, n_pre, *, ga_blk):
    t, d = x2.shape
    wa = y_a.shape[1]
    wsm = y_s.shape[1]
    tm = _tile(t, 256)
    row = lambda c: pl.BlockSpec((tm, c), lambda i: (i, 0))
    full = lambda a: pl.BlockSpec(a.shape, lambda i: (0, 0), pipeline_mode=pl.Buffered(1))
    return pl.pallas_call(
        _merge_kernel,
        grid=(t // tm,),
        in_specs=[row(d), row(wa), row(wsm),
                  pl.BlockSpec((tm, d), lambda i: (i, ga_blk)),
                  pl.BlockSpec((tm, d), lambda i: (i, ga_blk + 1)),
                  full(w_glu), full(b_glu), full(w_a), full(w_s), full(w_o),
                  full(n_post), full(n_pre)],
        out_specs=[row(d), row(d)],
        out_shape=[jax.ShapeDtypeStruct((t, d), F32),
                   jax.ShapeDtypeStruct((t, d), BF16)],
        compiler_params=_params("parallel"),
        name="merge",
    )(x2, y_a, y_s, proj, proj, w_glu, b_glu, w_a, w_s, w_o, n_post, n_pre)


def _ffn_up_kernel(z_ref, wg_ref, wu_ref, a_ref):
    z = z_ref[...]
    g = jnp.dot(z, wg_ref[...], preferred_element_type=F32)
    u = jnp.dot(z, wu_ref[...], preferred_element_type=F32)
    a_ref[...] = (jax.nn.silu(g) * u).astype(a_ref.dtype)


def _ffn_down_kernel(a_ref, h_ref, wd_ref, npost_ref, o_ref):
    f = jnp.dot(a_ref[...], wd_ref[...], preferred_element_type=F32)
    o_ref[...] = h_ref[...] + _rms(f, npost_ref[...])


def _ffn(z, h, w_g, w_u, w_d, n_post):
    t, d = z.shape
    f = w_g.shape[1]
    tm_up = _tile(t, 1024)
    tf = _tile(f, 512)
    a = pl.pallas_call(
        _ffn_up_kernel,
        grid=(t // tm_up, f // tf),
        in_specs=[pl.BlockSpec((tm_up, d), lambda i, j: (i, 0)),
                  pl.BlockSpec((d, tf), lambda i, j: (0, j)),
                  pl.BlockSpec((d, tf), lambda i, j: (0, j))],
        out_specs=pl.BlockSpec((tm_up, tf), lambda i, j: (i, j)),
        out_shape=jax.ShapeDtypeStruct((t, f), BF16),
        compiler_params=_params("parallel", "parallel"),
        name="ffn_up",
    )(z, w_g, w_u)
    tm_dn = _tile(t, 256)
    return pl.pallas_call(
        _ffn_down_kernel,
        grid=(t // tm_dn,),
        in_specs=[pl.BlockSpec((tm_dn, f), lambda i: (i, 0)),
                  pl.BlockSpec((tm_dn, d), lambda i: (i, 0)),
                  pl.BlockSpec((f, d), lambda i: (0, 0), pipeline_mode=pl.Buffered(1)),
                  pl.BlockSpec((1, d), lambda i: (0, 0))],
        out_specs=pl.BlockSpec((tm_dn, d), lambda i: (i, 0)),
        out_shape=jax.ShapeDtypeStruct((t, d), F32),
        compiler_params=_params("parallel"),
        name="ffn_down",
    )(a, h, w_d, n_post)


def _lambda_init(layer_idx):
    return 0.8 - 0.6 * math.exp(-0.3 * layer_idx)


def kernel(x, w_in, lambda_q1, lambda_k1, lambda_q2, lambda_k2, subln_w, ssm_a_re, ssm_a_im, ssm_log_dt, ssm_b_re, ssm_b_im, ssm_c_re, ssm_c_im, ssm_d, w_glu, b_glu, w_attn_branch, w_ssm_branch, w_out, norm_mix_pre, norm_mix_post, w_ffn_gate, w_ffn_up, w_ffn_down, norm_ffn_pre, norm_ffn_post):
    bsz, seq, d = x.shape
    depth = w_in.shape[0]
    dh = lambda_q1.shape[-1]
    e = 2 * dh
    attn_w = w_attn_branch.shape[1]
    ssm_w = w_ssm_branch.shape[1]
    n_heads = attn_w // e
    jb = ssm_w // LANES
    tc = SSM_CHUNK
    nc = seq // tc
    seg_rows = nc // SCAN_ROWS
    t = bsz * seq
    row = lambda v: v.reshape(1, -1).astype(F32)

    tm = _tile(seq, 512)
    qscale = dh ** -0.5 * math.log2(math.e)

    h = x.reshape(t, d)
    for l in range(depth):
        lam_init = _lambda_init(l)
        wq, wk, wv, wsi, wga, wgs = jnp.split(
            w_in[l], [attn_w, 2 * attn_w, 3 * attn_w, 3 * attn_w + ssm_w, 3 * attn_w + ssm_w + d],
            axis=1)
        w_main = jnp.concatenate([wq * qscale, wk, wga, wgs], axis=1).astype(BF16)
        proj, vt3, u = _in_proj(h, row(norm_mix_pre[l]), w_main, wv.T.astype(BF16),
                                wsi.astype(BF16), tm=tm, tn=d, e=e)
        y_a = _attention(proj.reshape(bsz, seq, -1), vt3, row(lambda_q1[l]), row(lambda_k1[l]),
                         row(lambda_q2[l]), row(lambda_k2[l]),
                         subln_w[l].reshape(e, 1).astype(F32),
                         n_heads=n_heads, dh=dh, lam_init=lam_init, tq=tm)
        y_a = y_a.reshape(t, attn_w)

        tables = _ssm_tables(ssm_a_re[l], ssm_a_im[l], ssm_log_dt[l], ssm_b_re[l],
                             ssm_b_im[l], ssm_c_re[l], ssm_c_im[l], ssm_d[l],
                             seg_rows=seg_rows)
        y_s = _ssm(u, tables, bsz=bsz, seg_rows=seg_rows)
        y_s = y_s.reshape(t, ssm_w)

        h1, z = _merge(h, y_a, y_s, proj, w_glu[l].astype(BF16), row(b_glu[l]),
                       w_attn_branch[l].astype(BF16), w_ssm_branch[l].astype(BF16),
                       w_out[l].astype(BF16), row(norm_mix_post[l]), row(norm_ffn_pre[l]),
                       ga_blk=2 * attn_w // d)
        h = _ffn(z, h1, w_ffn_gate[l].astype(BF16), w_ffn_up[l].astype(BF16),
                 w_ffn_down[l].astype(BF16), row(norm_ffn_post[l]))
    return h.reshape(bsz, seq, d)
```

```python
import functools
import math

import jax
import jax.numpy as jnp
from jax import lax
from jax.experimental import pallas as pl
from jax.experimental.pallas import tpu as pltpu

EPS = 1e-6
NEG_INF = -1e30
LANES = 128
SSM_CHUNK = 8
SCAN_ROWS = 8
ONES_ROWS = 16
VMEM_LIMIT_BYTES = 56 * 1024 * 1024
F32 = jnp.float32
BF16 = jnp.bfloat16
_NT = (((1,), (1,)), ((), ()))


def _params(*sem):
    return pltpu.CompilerParams(dimension_semantics=sem,
                                vmem_limit_bytes=VMEM_LIMIT_BYTES)


def _rms(x, gain):
    ms = jnp.mean(x * x, axis=-1, keepdims=True)
    return x * lax.rsqrt(ms + EPS) * gain


def _tile(n, pref):
    t = min(n, pref)
    assert n % t == 0, (n, t)
    return t


def _in_proj_kernel(x_ref, g_ref, w_ref, wvt_ref, ws_ref, o_ref, vt_ref, uc_ref,
                    u_ref, s_ref, *, e):
    @pl.when(pl.program_id(1) == 0)
    def _():
        u = _rms(x_ref[...], g_ref[...]).astype(u_ref.dtype)
        u_ref[...] = u
        vt = lax.dot_general(wvt_ref[...], u, _NT, preferred_element_type=F32)
        ea = e + ONES_ROWS
        for hd in range(vt.shape[0] // e):
            vt_ref[hd * ea:hd * ea + e, :] = vt[hd * e:(hd + 1) * e, :].astype(vt_ref.dtype)
            vt_ref[hd * ea + e:(hd + 1) * ea, :] = jnp.ones((ONES_ROWS, vt.shape[1]), vt_ref.dtype)
        s = jnp.dot(u, ws_ref[...], preferred_element_type=F32)
        nrow = uc_ref.shape[1]
        for j in range(s_ref.shape[0]):
            s_ref[j] = s[:, j * LANES:(j + 1) * LANES]
        for j in range(s_ref.shape[0]):
            for k in range(SSM_CHUNK):
                uc_ref[j, :, k * LANES:(k + 1) * LANES] = (
                    s_ref.at[j][pl.ds(k, nrow, stride=SSM_CHUNK), :].astype(uc_ref.dtype))

    o_ref[...] = jnp.dot(u_ref[...], w_ref[...],
                         preferred_element_type=F32).astype(o_ref.dtype)


def _in_proj(x2, gain, w, wvt, w_s, *, tm, tn, e):
    t, d = x2.shape
    n = w.shape[1]
    nva = wvt.shape[0] // e * (e + ONES_ROWS)
    jb = w_s.shape[1] // LANES
    kern = functools.partial(_in_proj_kernel, e=e)
    const = lambda a: pl.BlockSpec(a.shape, lambda i, j: (0, 0), pipeline_mode=pl.Buffered(1))
    return pl.pallas_call(
        kern,
        grid=(t // tm, n // tn),
        in_specs=[pl.BlockSpec((tm, d), lambda i, j: (i, 0)),
                  pl.BlockSpec((1, d), lambda i, j: (0, 0)),
                  pl.BlockSpec((d, tn), lambda i, j: (0, j)),
                  const(wvt), const(w_s)],
        out_specs=[pl.BlockSpec((tm, tn), lambda i, j: (i, j)),
                   pl.BlockSpec((None, nva, tm), lambda i, j: (i, 0, 0)),
                   pl.BlockSpec((jb, tm // SSM_CHUNK, SSM_CHUNK * LANES), lambda i, j: (0, i, 0))],
        out_shape=[jax.ShapeDtypeStruct((t, n), BF16),
                   jax.ShapeDtypeStruct((t // tm, nva, tm), BF16),
                   jax.ShapeDtypeStruct((jb, t // SSM_CHUNK, SSM_CHUNK * LANES), BF16)],
        scratch_shapes=[pltpu.VMEM((tm, d), BF16),
                        pltpu.VMEM((jb, tm, LANES), F32)],
        compiler_params=_params("parallel", "arbitrary"),
        name="in_proj",
    )(x2, gain, w, wvt, w_s)


def _attn_kernel(q_ref, k_ref, vt_ref, lq1_ref, lk1_ref, lq2_ref, lk2_ref, sw_ref,
                 o_ref, qs_ref, qns_ref, m_ref, acc_ref, sa_ref, sb_ref, sc_ref,
                 *, tq, tk, dh, lam_init):
    nq = q_ref.shape[0] // tq
    lax.fori_loop(0, nq, functools.partial(
        _attn_q_tile, q_ref, k_ref, vt_ref, lq1_ref, lk1_ref, lq2_ref, lk2_ref, sw_ref,
        o_ref, qs_ref, qns_ref, m_ref, acc_ref, sa_ref, sb_ref, sc_ref,
        nq=nq, tq=tq, tk=tk, dh=dh, lam_init=lam_init), 0)


def _attn_q_tile(q_ref, k_ref, vt_ref, lq1_ref, lk1_ref, lq2_ref, lk2_ref, sw_ref,
                 o_ref, qs_ref, qns_ref, m_ref, acc_ref, sa_ref, sb_ref, sc_ref,
                 qi, carry, *, nq, tq, tk, dh, lam_init):
    def stack(tile, dst_ref):
        rows = pl.ds(pl.multiple_of(tile * tq, tq), tq)
        qt = q_ref[rows, :].astype(F32).T
        row = lax.broadcasted_iota(jnp.int32, qt.shape, 0)
        zero = jnp.zeros_like(qt)
        dst_ref[:, 0:tq] = jnp.where(row < dh, qt, zero).astype(dst_ref.dtype)
        dst_ref[:, tq:2 * tq] = jnp.where(row >= dh, qt, zero).astype(dst_ref.dtype)

    stack(qi, qs_ref)
    stack(jnp.minimum(qi + 1, nq - 1), qns_ref)
    m_ref[...] = jnp.full(m_ref.shape, NEG_INF, F32)
    acc_ref[...] = jnp.zeros(acc_ref.shape, F32)

    def scores(ki, s_ref, stacked_q_ref=qs_ref):
        k = k_ref[pl.ds(pl.multiple_of(ki * tk, tk), tk), :]
        s_ref[...] = jnp.dot(k, stacked_q_ref[...],
                             preferred_element_type=F32)

    def scores_next():
        scores(0, sc_ref, qns_ref)

    def consume(ki, s_ref, masked):
        s = s_ref[...]
        if masked:
            kpos = ki * tk + lax.broadcasted_iota(jnp.int32, s.shape, 0)
            col = lax.broadcasted_iota(jnp.int32, s.shape, 1)
            qpos = qi * tq + jnp.where(col >= tq, col - tq, col)
            s = jnp.where(kpos <= qpos, s, NEG_INF)
        m_prev = m_ref[...]
        m_new = jnp.maximum(m_prev, jnp.max(s, axis=0, keepdims=True))
        alpha = jnp.exp2(m_prev - m_new)
        p = jnp.exp2(s - m_new)
        acc_ref[...] = alpha * acc_ref[...] + jnp.dot(
            vt_ref[ki], p.astype(vt_ref.dtype), preferred_element_type=F32)
        m_ref[...] = m_new

    @pl.when(qi == 0)
    def _():
        scores(0, sa_ref)
        scores_next()
        consume(0, sa_ref, True)

    @pl.when(qi > 0)
    def _():
        scores(1, sb_ref)
        consume(0, sc_ref, False)

        def pair(t):
            scores(t + 1, sa_ref)
            consume(t, sb_ref, False)
            scores(t + 2, sb_ref)
            consume(t + 1, sa_ref, False)

        def quad(j, carry):
            pair(4 * j + 1)
            pair(4 * j + 3)
            return carry

        nquad = (qi - 1) // 4
        lax.fori_loop(0, nquad, quad, 0)

        @pl.when((qi - 1) % 4 >= 2)
        def _():
            pair(4 * nquad + 1)

        @pl.when(qi % 2 == 1)
        def _():
            scores_next()
            consume(qi, sb_ref, True)

        @pl.when(qi % 2 == 0)
        def _():
            scores(qi, sa_ref)
            consume(qi - 1, sb_ref, False)
            scores_next()
            consume(qi, sa_ref, True)

    lam = (jnp.exp(jnp.sum(lq1_ref[...] * lk1_ref[...], axis=-1, keepdims=True))
           - jnp.exp(jnp.sum(lq2_ref[...] * lk2_ref[...], axis=-1, keepdims=True))
           + lam_init)
    e = 2 * dh
    acc = acc_ref[...]
    o_all = acc[0:e, :] / acc[e:e + 1, :]
    o = o_all[:, 0:tq] - lam * o_all[:, tq:2 * tq]
    ms = jnp.mean(o * o, axis=0, keepdims=True)
    o = o * lax.rsqrt(ms + EPS) * sw_ref[...] * (1.0 - lam_init)
    o_ref[pl.ds(pl.multiple_of(qi * tq, tq), tq), :] = o.T.astype(o_ref.dtype)
    return carry


def _attention(proj3, vt3, lq1, lk1, lq2, lk2, subln, *, n_heads, dh, lam_init, tq):
    bsz, seq, _ = proj3.shape
    e = 2 * dh
    nk = seq // tq
    ea = e + ONES_ROWS
    assert vt3.shape == (bsz * nk, n_heads * ea, tq)
    kern = functools.partial(_attn_kernel, tq=tq, tk=tq, dh=dh, lam_init=lam_init)
    vec = pl.BlockSpec((1, dh), lambda b, h: (0, 0))
    return pl.pallas_call(
        kern,
        grid=(bsz, n_heads),
        in_specs=[pl.BlockSpec((None, seq, e), lambda b, h: (b, 0, h)),
                  pl.BlockSpec((None, seq, e), lambda b, h: (b, 0, n_heads + h)),
                  pl.BlockSpec((nk, ea, tq), lambda b, h: (b, h, 0)),
                  vec, vec, vec, vec,
                  pl.BlockSpec((e, 1), lambda b, h: (0, 0))],
        out_specs=pl.BlockSpec((None, seq, e), lambda b, h: (b, 0, h)),
        out_shape=jax.ShapeDtypeStruct((bsz, seq, n_heads * e), BF16),
        scratch_shapes=[pltpu.VMEM((e, 2 * tq), BF16),
                        pltpu.VMEM((e, 2 * tq), BF16),
                        pltpu.VMEM((1, 2 * tq), F32),
                        pltpu.VMEM((ea, 2 * tq), F32),
                        pltpu.VMEM((tq, 2 * tq), F32),
                        pltpu.VMEM((tq, 2 * tq), F32),
                        pltpu.VMEM((tq, 2 * tq), F32)],
        compiler_params=_params("parallel", "parallel"),
        name="diff_attention",
    )(proj3, proj3, vt3, lq1, lk1, lq2, lk2, subln)


def _ssm_tables(a_re, a_im, log_dt, b_re, b_im, c_re, c_im, d_skip, *, seg_rows):
    g, p, h = b_re.shape
    tc = SSM_CHUNK
    gpb = LANES // h
    jb = g // gpb
    dt = jnp.exp(log_dt)[:, None]
    er, ei = a_re * dt, a_im * dt

    def powers(n):
        n = jnp.asarray(n, F32).reshape((-1, 1, 1))
        mag = jnp.exp(n * er)
        return mag * jnp.cos(n * ei), mag * jnp.sin(n * ei)

    ar, ai = powers([1.0])
    ar, ai = ar[0], ai[0]
    den = a_re * a_re + a_im * a_im
    fr = ((ar - 1.0) * a_re + ai * a_im) / den
    fi = (ai * a_re - (ar - 1.0) * a_im) / den
    bbr = fr[..., None] * b_re - fi[..., None] * b_im
    bbi = fr[..., None] * b_im + fi[..., None] * b_re

    pr, pi = powers(jnp.arange(tc + 1))
    cpr = c_re[None] * pr[:, :, None, :] - c_im[None] * pi[:, :, None, :]
    cpi = c_re[None] * pi[:, :, None, :] + c_im[None] * pr[:, :, None, :]
    kern = (jnp.einsum('tgop,gpi->tgio', cpr[:tc], bbr)
            - jnp.einsum('tgop,gpi->tgio', cpi[:tc], bbi))
    def group_mask(rows_per_group, cols_per_group):
        r = jnp.arange(gpb * rows_per_group) // rows_per_group
        c = jnp.arange(gpb * cols_per_group) // cols_per_group
        return r[:, None] == c[None, :]

    def block_diag(x, rows_per_group, cols_per_group):
        x = jnp.tile(x, (1,) * (x.ndim - 1) + (gpb,))
        return jnp.where(group_mask(rows_per_group, cols_per_group), x, 0.0)

    kidx = jnp.arange(tc)
    tau = kidx[None, :] - kidx[:, None]
    bd = block_diag(kern.reshape(tc, jb, LANES, h), h, h)
    kt = jnp.where((tau >= 0)[:, :, None, None, None], bd[jnp.clip(tau, 0)], 0.0)
    t_mat = jnp.transpose(kt, (2, 0, 3, 1, 4)).reshape(jb, tc * LANES, tc * LANES)

    qr, qi_ = pr[tc - 1 - kidx], pi[tc - 1 - kidx]
    sr = qr[:, :, None, :] * jnp.swapaxes(bbr, 1, 2)[None] - qi_[:, :, None, :] * jnp.swapaxes(bbi, 1, 2)[None]
    si = qr[:, :, None, :] * jnp.swapaxes(bbi, 1, 2)[None] + qi_[:, :, None, :] * jnp.swapaxes(bbr, 1, 2)[None]
    st = jnp.stack([sr, si], 0).reshape(2, tc, jb, LANES, p)
    st = block_diag(st, h, p)
    w_st = jnp.transpose(st, (2, 1, 3, 0, 4)).reshape(jb, tc * LANES, 2 * gpb * p)

    ot = jnp.stack([cpr[1:], -cpi[1:]], 0)
    ot = jnp.swapaxes(ot, 3, 4).reshape(2, tc, jb, gpb * p, h)
    ot = block_diag(ot, p, h)
    w_out = jnp.transpose(ot, (2, 0, 3, 1, 4)).reshape(jb, 2 * gpb * p, tc * LANES)

    def rows(n):
        xr, xi = powers(n)
        x = jnp.stack([xr, xi], 1).reshape(-1, 2, jb, gpb * p)
        return jnp.transpose(x, (2, 0, 1, 3)).reshape(jb, -1, 2 * gpb * p)

    a_chunk = rows([float(tc)])
    a_seg = rows([float(tc * seg_rows)])
    p_pow = rows(tc * jnp.arange(seg_rows))
    d_row = jnp.tile(d_skip.reshape(jb, 1, LANES), (1, 1, tc))
    return (t_mat.astype(BF16), w_st.astype(BF16), w_out.astype(BF16),
            a_chunk, a_seg, p_pow, d_row)


def _segment_pitch(seg_rows):
    return seg_rows + SCAN_ROWS


def _ssm_kernel(u_ref, t_ref, wst_ref, wout_ref, ach_ref, aseg_ref, pp_ref, d_ref,
                y_ref, xst_ref, e_ref, sp_ref, z_ref, *, seg_rows):
    ncb = xst_ref.shape[0]
    hb = ncb // 2
    pitch = _segment_pitch(seg_rows)
    blk = lambda c: slice(c * LANES, (c + 1) * LANES)
    u = u_ref[...]
    xst = jnp.dot(u, wst_ref[...], preferred_element_type=F32)
    for c in range(ncb):
        for s in range(SCAN_ROWS):
            xst_ref[c, s * pitch:s * pitch + seg_rows, :] = (
                xst[s * seg_rows:(s + 1) * seg_rows, blk(c)])

    ach = ach_ref[...]
    a_ch = [jnp.broadcast_to(ach[:, blk(c)], (SCAN_ROWS, LANES)) for c in range(ncb)]

    def cstep(a, e, c):
        return (a[c] * e[c] - a[c + hb] * e[c + hb], a[c] * e[c + hb] + a[c + hb] * e[c])

    def scan_body(i, e):
        row0 = pl.multiple_of(i * SCAN_ROWS, SCAN_ROWS)
        new = [None] * ncb
        for c in range(hb):
            e_ref[c, pl.ds(row0, SCAN_ROWS), :] = e[c]
            e_ref[c + hb, pl.ds(row0, SCAN_ROWS), :] = e[c + hb]
            nr, ni = cstep(a_ch, e, c)
            new[c] = nr + xst_ref.at[c][pl.ds(i, SCAN_ROWS, stride=pitch), :]
            new[c + hb] = ni + xst_ref.at[c + hb][pl.ds(i, SCAN_ROWS, stride=pitch), :]
        return tuple(new)

    zero = jnp.zeros((SCAN_ROWS, LANES), F32)
    end = lax.fori_loop(0, seg_rows, scan_body, (zero,) * ncb, unroll=4)

    asg = aseg_ref[...]
    a_sg = [asg[:, blk(c)] for c in range(ncb)]
    car = [jnp.zeros((1, LANES), F32)] * ncb
    pp = pp_ref[...]
    p_pw = [pp[:, blk(c)] for c in range(ncb)]
    for s in range(SCAN_ROWS):
        rows = slice(s * seg_rows, (s + 1) * seg_rows)
        nxt = [None] * ncb
        for c in range(hb):
            fr, fi = cstep(p_pw, car, c)
            sp_ref[rows, blk(c)] = (
                e_ref.at[c][pl.ds(s, seg_rows, stride=SCAN_ROWS), :] + fr).astype(sp_ref.dtype)
            sp_ref[rows, blk(c + hb)] = (
                e_ref.at[c + hb][pl.ds(s, seg_rows, stride=SCAN_ROWS), :] + fi).astype(sp_ref.dtype)
            nr, ni = cstep(a_sg, car, c)
            nxt[c] = nr + end[c][s:s + 1, :]
            nxt[c + hb] = ni + end[c + hb][s:s + 1, :]
        car = nxt

    y = jnp.dot(u, t_ref[...], preferred_element_type=F32)
    y = y + jnp.dot(sp_ref[...], wout_ref[...], preferred_element_type=F32)
    y = y + d_ref[...] * u.astype(F32)

    tc = SSM_CHUNK
    ng = y.shape[0] // SCAN_ROWS
    for k in range(tc):
        z_ref[:, k * SCAN_ROWS:(k + 1) * SCAN_ROWS, :] = (
            y[:, blk(k)].reshape(ng, SCAN_ROWS, LANES))
    for c in range(SCAN_ROWS):
        y_ref[:, c * tc:(c + 1) * tc, :] = z_ref[:, pl.ds(c, tc, stride=SCAN_ROWS), :]


def _ssm(u, tables, *, bsz, seg_rows):
    t_mat, w_st, w_out, a_chunk, a_seg, p_pow, d_row = tables
    jb, m, kc = u.shape
    nc = m // bsz
    s2 = w_st.shape[2]
    assert nc == SCAN_ROWS * seg_rows
    kern = functools.partial(_ssm_kernel, seg_rows=seg_rows)
    wspec = lambda r, c: pl.BlockSpec((None, r, c), lambda j, b: (j, 0, 0))
    return pl.pallas_call(
        kern,
        grid=(jb, bsz),
        in_specs=[pl.BlockSpec((None, nc, kc), lambda j, b: (j, b, 0)),
                  wspec(kc, kc), wspec(kc, s2), wspec(s2, kc),
                  wspec(1, s2), wspec(1, s2), wspec(seg_rows, s2), wspec(1, kc)],
        out_specs=pl.BlockSpec((nc // SCAN_ROWS, SCAN_ROWS * SSM_CHUNK, LANES),
                               lambda j, b: (b, 0, j)),
        out_shape=jax.ShapeDtypeStruct((m // SCAN_ROWS, SCAN_ROWS * SSM_CHUNK, jb * LANES), F32),
        scratch_shapes=[pltpu.VMEM((s2 // LANES, SCAN_ROWS * _segment_pitch(seg_rows), LANES), F32),
                        pltpu.VMEM((s2 // LANES, nc, LANES), F32),
                        pltpu.VMEM((nc, s2), BF16),
                        pltpu.VMEM((nc // SCAN_ROWS, SCAN_ROWS * SSM_CHUNK, LANES), F32)],
        compiler_params=_params("parallel", "arbitrary"),
        name="s5_scan",
    )(u, t_mat, w_st, w_out, a_chunk, a_seg, p_pow, d_row)


def _merge_kernel(x_ref, ya_ref, ys_ref, ga_ref, gs_ref, wglu_ref, bglu_ref, wa_ref,
                  ws_ref, wo_ref, npost_ref, npre_ref, h_ref, z_ref):
    ys = jax.nn.gelu(ys_ref[...].astype(F32), approximate=True)
    gate = jnp.dot(ys.astype(BF16), wglu_ref[...], preferred_element_type=F32)
    ys = ys * jax.nn.sigmoid(gate + bglu_ref[...])
    ma = jnp.dot(ya_ref[...], wa_ref[...], preferred_element_type=F32)
    ms = jnp.dot(ys.astype(BF16), ws_ref[...], preferred_element_type=F32)
    merged = (jax.nn.sigmoid(ga_ref[...].astype(F32)) * ma
              + jax.nn.sigmoid(gs_ref[...].astype(F32)) * ms)
    mix = jnp.dot(merged.astype(BF16), wo_ref[...], preferred_element_type=F32)
    h = x_ref[...] + _rms(mix, npost_ref[...])
    h_ref[...] = h
    z_ref[...] = _rms(h, npre_ref[...]).astype(z_ref.dtype)


def _merge(x2, y_a, y_s, proj, w_glu, b_glu, w_a, w_s, w_o, n_post, n_pre, *, ga_blk):
    t, d = x2.shape
    wa = y_a.shape[1]
    wsm = y_s.shape[1]
    tm = _tile(t, 256)
    row = lambda c: pl.BlockSpec((tm, c), lambda i: (i, 0))
    full = lambda a: pl.BlockSpec(a.shape, lambda i: (0, 0), pipeline_mode=pl.Buffered(1))
    return pl.pallas_call(
        _merge_kernel,
        grid=(t // tm,),
        in_specs=[row(d), row(wa), row(wsm),
                  pl.BlockSpec((tm, d), lambda i: (i, ga_blk)),
                  pl.BlockSpec((tm, d), lambda i: (i, ga_blk + 1)),
                  full(w_glu), full(b_glu), full(w_a), full(w_s), full(w_o),
                  full(n_post), full(n_pre)],
        out_specs=[row(d), row(d)],
        out_shape=[jax.ShapeDtypeStruct((t, d), F32),
                   jax.ShapeDtypeStruct((t, d), BF16)],
        compiler_params=_params("parallel"),
        name="merge",
    )(x2, y_a, y_s, proj, proj, w_glu, b_glu, w_a, w_s, w_o, n_post, n_pre)


def _ffn_up_kernel(z_ref, wg_ref, wu_ref, a_ref):
    z = z_ref[...]
    g = jnp.dot(z, wg_ref[...], preferred_element_type=F32)
    u = jnp.dot(z, wu_ref[...], preferred_element_type=F32)
    a_ref[...] = (jax.nn.silu(g) * u).astype(a_ref.dtype)


def _ffn_down_kernel(a_ref, h_ref, wd_ref, npost_ref, o_ref):
    f = jnp.dot(a_ref[...], wd_ref[...], preferred_element_type=F32)
    o_ref[...] = h_ref[...] + _rms(f, npost_ref[...])


def _ffn(z, h, w_g, w_u, w_d, n_post):
    t, d = z.shape
    f = w_g.shape[1]
    tm_up = _tile(t, 1024)
    tf = _tile(f, 512)
    a = pl.pallas_call(
        _ffn_up_kernel,
        grid=(t // tm_up, f // tf),
        in_specs=[pl.BlockSpec((tm_up, d), lambda i, j: (i, 0)),
                  pl.BlockSpec((d, tf), lambda i, j: (0, j)),
                  pl.BlockSpec((d, tf), lambda i, j: (0, j))],
        out_specs=pl.BlockSpec((tm_up, tf), lambda i, j: (i, j)),
        out_shape=jax.ShapeDtypeStruct((t, f), BF16),
        compiler_params=_params("parallel", "parallel"),
        name="ffn_up",
    )(z, w_g, w_u)
    tm_dn = _tile(t, 512)
    return pl.pallas_call(
        _ffn_down_kernel,
        grid=(t // tm_dn,),
        in_specs=[pl.BlockSpec((tm_dn, f), lambda i: (i, 0)),
                  pl.BlockSpec((tm_dn, d), lambda i: (i, 0)),
                  pl.BlockSpec((f, d), lambda i: (0, 0), pipeline_mode=pl.Buffered(1)),
                  pl.BlockSpec((1, d), lambda i: (0, 0))],
        out_specs=pl.BlockSpec((tm_dn, d), lambda i: (i, 0)),
        out_shape=jax.ShapeDtypeStruct((t, d), F32),
        compiler_params=_params("parallel"),
        name="ffn_down",
    )(a, h, w_d, n_post)


def _lambda_init(layer_idx):
    return 0.8 - 0.6 * math.exp(-0.3 * layer_idx)


def kernel(x, w_in, lambda_q1, lambda_k1, lambda_q2, lambda_k2, subln_w, ssm_a_re, ssm_a_im, ssm_log_dt, ssm_b_re, ssm_b_im, ssm_c_re, ssm_c_im, ssm_d, w_glu, b_glu, w_attn_branch, w_ssm_branch, w_out, norm_mix_pre, norm_mix_post, w_ffn_gate, w_ffn_up, w_ffn_down, norm_ffn_pre, norm_ffn_post):
    bsz, seq, d = x.shape
    depth = w_in.shape[0]
    dh = lambda_q1.shape[-1]
    e = 2 * dh
    attn_w = w_attn_branch.shape[1]
    ssm_w = w_ssm_branch.shape[1]
    n_heads = attn_w // e
    jb = ssm_w // LANES
    tc = SSM_CHUNK
    nc = seq // tc
    seg_rows = nc // SCAN_ROWS
    t = bsz * seq
    row = lambda v: v.reshape(1, -1).astype(F32)

    tm = _tile(seq, 512)
    qscale = dh ** -0.5 * math.log2(math.e)

    h = x.reshape(t, d)
    for l in range(depth):
        lam_init = _lambda_init(l)
        wq, wk, wv, wsi, wga, wgs = jnp.split(
            w_in[l], [attn_w, 2 * attn_w, 3 * attn_w, 3 * attn_w + ssm_w, 3 * attn_w + ssm_w + d],
            axis=1)
        w_main = jnp.concatenate([wq * qscale, wk, wga, wgs], axis=1).astype(BF16)
        proj, vt3, u = _in_proj(h, row(norm_mix_pre[l]), w_main, wv.T.astype(BF16),
                                wsi.astype(BF16), tm=tm, tn=d, e=e)
        y_a = _attention(proj.reshape(bsz, seq, -1), vt3, row(lambda_q1[l]), row(lambda_k1[l]),
                         row(lambda_q2[l]), row(lambda_k2[l]),
                         subln_w[l].reshape(e, 1).astype(F32),
                         n_heads=n_heads, dh=dh, lam_init=lam_init, tq=tm)
        y_a = y_a.reshape(t, attn_w)

        tables = _ssm_tables(ssm_a_re[l], ssm_a_im[l], ssm_log_dt[l], ssm_b_re[l],
                             ssm_b_im[l], ssm_c_re[l], ssm_c_im[l], ssm_d[l],
                             seg_rows=seg_rows)
        y_s = _ssm(u, tables, bsz=bsz, seg_rows=seg_rows)
        y_s = y_s.reshape(t, ssm_w)

        h1, z = _merge(h, y_a, y_s, proj, w_glu[l].astype(BF16), row(b_glu[l]),
                       w_attn_branch[l].astype(BF16), w_ssm_branch[l].astype(BF16),
                       w_out[l].astype(BF16), row(norm_mix_post[l]), row(norm_ffn_pre[l]),
                       ga_blk=2 * attn_w // d)
        h = _ffn(z, h1, w_ffn_gate[l].astype(BF16), w_ffn_up[l].astype(BF16),
                 w_ffn_down[l].astype(BF16), row(norm_ffn_post[l]))
    return h.reshape(bsz, seq, d)
```
